```python
import math
import jax, jax.numpy as jnp
from jax import lax
import numpy as np

D_MODEL = 1024
BATCH = 4
SEQ = 8192
DEPTH = 1

N_META = 16
CHUNK = 128
PAD = CHUNK - N_META
RET_HEADS = 4
RET_DK = D_MODEL // RET_HEADS
RET_DV = 2 * RET_DK
RET_THETA = 10000.0
DIFF_HEADS = 8
DIFF_DH = D_MODEL // (2 * DIFF_HEADS)
DIFF_DV = 2 * DIFF_DH
ROPE_THETA = 500000.0
ROPE_DIM = DIFF_DH // 4
FFN_DIM = 2816
CONV_W = 3
EPS = 1e-6
NEG = -1e30

RQ = RET_HEADS * RET_DK
RK = RET_HEADS * RET_DK
RV = RET_HEADS * RET_DV
RG = RET_HEADS * RET_DV
DQ = DIFF_HEADS * 2 * DIFF_DH
DK = DIFF_HEADS * 2 * DIFF_DH
DV = DIFF_HEADS * DIFF_DV
GATE_COLS = 2 * D_MODEL
IN_COLS = RQ + RK + RV + RG + DQ + DK + DV + GATE_COLS
SPLIT_IDX = [int(s) for s in np.cumsum([RQ, RK, RV, RG, DQ, DK, DV])]

kernel_name = "hybrid_retention_diffattn_convffn"


def rmsnorm(x, w):
    xf = x.astype(jnp.float32)
    y = xf * lax.rsqrt(jnp.mean(xf * xf, axis=-1, keepdims=True) + EPS)
    return (y * w.astype(jnp.float32)).astype(x.dtype)


def rope(x, pos, rot_dim, theta):
    half = rot_dim // 2
    inv = jnp.power(theta, -jnp.arange(half, dtype=jnp.float32) / half)
    ang = pos[:, None] * inv[None, :]
    shape = (1, pos.shape[0]) + (1,) * (x.ndim - 3) + (half,)
    cos = jnp.cos(ang).reshape(shape)
    sin = jnp.sin(ang).reshape(shape)
    xr = x[..., :rot_dim].astype(jnp.float32)
    x1, x2 = xr[..., :half], xr[..., half:]
    rot = jnp.concatenate([x1 * cos - x2 * sin, x2 * cos + x1 * sin], axis=-1).astype(x.dtype)
    return jnp.concatenate([rot, x[..., rot_dim:]], axis=-1)


def retention(q, k, v):
    B, Lp, H, dk = q.shape
    dv = v.shape[-1]
    n = Lp // CHUNK

    def to_chunks(t):
        return t.reshape(B, n, CHUNK, H, t.shape[-1]).transpose(1, 0, 3, 2, 4)

    lg = jnp.log(1.0 - jnp.power(2.0, -5.0 - jnp.arange(H, dtype=jnp.float32)))
    idx = jnp.arange(CHUNK, dtype=jnp.float32)
    dist = idx[:, None] - idx[None, :]
    intra = jnp.where(dist[None] >= 0, jnp.exp(lg[:, None, None] * jnp.maximum(dist, 0.0)[None]), 0.0)
    q_dec = jnp.exp(lg[:, None] * (idx[None] + 1.0))[:, :, None]
    k_dec = jnp.exp(lg[:, None] * (CHUNK - 1.0 - idx[None]))[:, :, None]
    chunk_dec = jnp.exp(lg * CHUNK)[:, None, None]

    def step(R, qkv):
        qc, kc, vc = qkv
        s = jnp.einsum('bhid,bhjd->bhij', qc, kc) * intra
        o = jnp.einsum('bhij,bhje->bhie', s, vc) + jnp.einsum('bhid,bhde->bhie', qc, R) * q_dec
        R = R * chunk_dec + jnp.einsum('bhjd,bhje->bhde', kc * k_dec, vc)
        return R, o

    R0 = jnp.zeros((B, H, dk, dv), jnp.float32)
    _, o = lax.scan(step, R0, (to_chunks(q), to_chunks(k), to_chunks(v)))
    return o.transpose(1, 0, 3, 2, 4).reshape(B, Lp, H, dv)


def diff_attention(q, k, v, lam):
    B, Lp, H, _, dh = q.shape
    nb = Lp // CHUNK
    qb = q.reshape(B, nb, CHUNK, H, 2, dh).transpose(1, 0, 3, 4, 2, 5)
    kt = k.transpose(0, 2, 3, 1, 4)
    vt = v.transpose(0, 2, 1, 3)
    kpos = jnp.arange(Lp)
    scale = dh ** -0.5

    def block(args):
        qblk, i = args
        qpos = i * CHUNK + jnp.arange(CHUNK)
        mask = (kpos[None, :] <= qpos[:, None]) & (kpos[None, :] >= PAD)
        s = jnp.einsum('bhcqd,bhckd->bhcqk', qblk, kt) * scale
        p = jax.nn.softmax(jnp.where(mask, s, NEG), axis=-1)
        a = p[:, :, 0] - lam * p[:, :, 1]
        return jnp.einsum('bhqk,bhke->bhqe', a, vt)

    o = lax.map(block, (qb, jnp.arange(nb)))
    return o.transpose(1, 0, 3, 2, 4).reshape(B, Lp, H, v.shape[-1])


def causal_dwconv(x, w, b):
    C = x.shape[-1]
    y = lax.conv_general_dilated(x, w.astype(x.dtype)[:, None, :], window_strides=(1,),
                                 padding=[(CONV_W - 1, 0)], dimension_numbers=('NWC', 'WIO', 'NWC'),
                                 feature_group_count=C)
    return y + b.astype(x.dtype)


def setup_inputs(seed: int = 0) -> dict:
    key = jax.random.key(seed)
    ks = jax.random.split(key, 20)
    f = jnp.float32
    nrm = lambda k, shape, s: jax.random.normal(k, shape, f) * s
    return {
        'x': nrm(ks[0], (BATCH, SEQ, D_MODEL), 1.0),
        'meta_tokens': nrm(ks[1], (N_META, D_MODEL), 1.0),
        'norm1_w': 1.0 + nrm(ks[2], (DEPTH, D_MODEL), 0.01),
        'w_in': nrm(ks[3], (DEPTH, D_MODEL, IN_COLS), D_MODEL ** -0.5),
        'w_ret_o': nrm(ks[4], (DEPTH, RV, D_MODEL), RV ** -0.5),
        'q_norm_w': 1.0 + nrm(ks[5], (DEPTH, DIFF_DH), 0.01),
        'k_norm_w': 1.0 + nrm(ks[6], (DEPTH, DIFF_DH), 0.01),
        'lambda_q1': nrm(ks[7], (DEPTH, DIFF_DH), 0.1),
        'lambda_k1': nrm(ks[8], (DEPTH, DIFF_DH), 0.1),
        'lambda_q2': nrm(ks[9], (DEPTH, DIFF_DH), 0.1),
        'lambda_k2': nrm(ks[10], (DEPTH, DIFF_DH), 0.1),
        'diff_subln_w': 1.0 + nrm(ks[11], (DEPTH, DIFF_DV), 0.01),
        'w_diff_o': nrm(ks[12], (DEPTH, DV, D_MODEL), DV ** -0.5),
        'w_out': nrm(ks[13], (DEPTH, D_MODEL, D_MODEL), D_MODEL ** -0.5),
        'norm2_w': 1.0 + nrm(ks[14], (DEPTH, D_MODEL), 0.01),
        'w_up': nrm(ks[15], (DEPTH, D_MODEL, 2 * FFN_DIM), D_MODEL ** -0.5),
        'conv_w': nrm(ks[16], (DEPTH, CONV_W, 2 * FFN_DIM), CONV_W ** -0.5),
        'conv_b': nrm(ks[17], (DEPTH, 2 * FFN_DIM), 0.01),
        'w_down': nrm(ks[18], (DEPTH, FFN_DIM, D_MODEL), FFN_DIM ** -0.5),
    }


def reference(x, meta_tokens, norm1_w, w_in, w_ret_o, q_norm_w, k_norm_w, lambda_q1, lambda_k1,
              lambda_q2, lambda_k2, diff_subln_w, w_diff_o, w_out, norm2_w, w_up, conv_w, conv_b, w_down):
    B, S, D = x.shape
    L = N_META + S
    Lp = PAD + L
    dt = x.dtype
    h = jnp.concatenate([jnp.broadcast_to(meta_tokens.astype(dt)[None], (B, N_META, D)), x], axis=1)
    pos = jnp.arange(Lp, dtype=jnp.float32) - PAD

    for l in range(DEPTH):
        lam_init = 0.8 - 0.6 * math.exp(-0.3 * l)
        u = rmsnorm(h, norm1_w[l])
        u_pad = jnp.pad(u, ((0, 0), (PAD, 0), (0, 0)))
        proj = u_pad @ w_in[l].astype(dt)
        rq, rk, rv, rg, dq, dk, dv, gates = jnp.split(proj, SPLIT_IDX, axis=-1)

        rq = rope(rq.reshape(B, Lp, RET_HEADS, RET_DK), pos, RET_DK, RET_THETA)
        rk = rope(rk.reshape(B, Lp, RET_HEADS, RET_DK), pos, RET_DK, RET_THETA) * (RET_DK ** -0.5)
        rv = rv.reshape(B, Lp, RET_HEADS, RET_DV)
        ro = retention(rq.astype(jnp.float32), rk.astype(jnp.float32), rv.astype(jnp.float32))
        mu = jnp.mean(ro, axis=-1, keepdims=True)
        var = jnp.mean(jnp.square(ro - mu), axis=-1, keepdims=True)
        ro = ((ro - mu) * lax.rsqrt(var + EPS)).astype(dt).reshape(B, Lp, RV)
        ro = (jax.nn.silu(rg) * ro) @ w_ret_o[l].astype(dt)

        dq = rope(rmsnorm(dq.reshape(B, Lp, DIFF_HEADS, 2, DIFF_DH), q_norm_w[l]), pos, ROPE_DIM, ROPE_THETA)
        dk = rope(rmsnorm(dk.reshape(B, Lp, DIFF_HEADS, 2, DIFF_DH), k_norm_w[l]), pos, ROPE_DIM, ROPE_THETA)
        dv = dv.reshape(B, Lp, DIFF_HEADS, DIFF_DV)
        lam = (jnp.exp(jnp.sum(lambda_q1[l].astype(jnp.float32) * lambda_k1[l].astype(jnp.float32)))
               - jnp.exp(jnp.sum(lambda_q2[l].astype(jnp.float32) * lambda_k2[l].astype(jnp.float32))) + lam_init)
        do = diff_attention(dq.astype(jnp.float32), dk.astype(jnp.float32), dv.astype(jnp.float32), lam)
        do = (rmsnorm(do, diff_subln_w[l]) * (1.0 - lam_init)).astype(dt).reshape(B, Lp, DV)
        do = do @ w_diff_o[l].astype(dt)

        g_ret, g_diff = jnp.split(gates, 2, axis=-1)
        mix = (jax.nn.sigmoid(g_ret) * ro + jax.nn.sigmoid(g_diff) * do) @ w_out[l].astype(dt)
        h = h + mix[:, PAD:]

        u = rmsnorm(h, norm2_w[l])
        up = causal_dwconv(u @ w_up[l].astype(dt), conv_w[l], conv_b[l])
        a, b = jnp.split(up, 2, axis=-1)
        h = h + (jax.nn.silu(a) * b) @ w_down[l].astype(dt)

    return h[:, N_META:]
```

```python
import functools
import math

import numpy as np
import jax
import jax.numpy as jnp
from jax import lax
from jax.experimental import pallas as pl
from jax.experimental.pallas import tpu as pltpu

F32 = jnp.float32
BF16 = jnp.bfloat16

N_META = 16
CHUNK = 128
PAD = CHUNK - N_META
RET_HEADS = 4
RET_THETA = 10000.0
DIFF_HEADS = 8
ROPE_THETA = 500000.0
CONV_W = 3
EPS = 1e-6
NEG = -1e30
LANE = 128
MXU = 256
VMEM_LIMIT = 56 * 1024 * 1024

ATT_TQ = 256


def _pick_tile(n, candidates):
    for c in candidates:
        if n % c == 0:
            return c
    raise ValueError(f"no tile for {n}")


def _dot(a, b):
    return jnp.dot(a, b, preferred_element_type=F32)


def _dot_nt(a, b):
    return lax.dot_general(a, b, (((1,), (1,)), ((), ())), preferred_element_type=F32)


def _proj_kernel(seg_of_tile, d_model, ret_dk, diff_dh, ret_k_scale, diff_q_scale,
                 x_ref, n1_ref, w_ref, rcos_ref, rsin_ref, dc_ref, ds1_ref, ds2_ref,
                 qw_ref, kw_ref, g_ref, o_ref, u_ref):
    j = pl.program_id(1)

    @pl.when(j == 0)
    def _():
        x = x_ref[...]
        ms = jnp.mean(x * x, axis=-1, keepdims=True)
        u_ref[...] = (x * lax.rsqrt(ms + EPS) * n1_ref[...]).astype(BF16)

    y = _dot(u_ref[...], w_ref[...])
    half = ret_dk // 2

    def ret_rope(scale):
        cos = rcos_ref[...]
        sin = rsin_ref[...]
        for h in range(d_model // ret_dk):
            x1 = y[:, h * ret_dk:h * ret_dk + half]
            x2 = y[:, h * ret_dk + half:(h + 1) * ret_dk]
            o_ref[:, h * ret_dk:h * ret_dk + half] = ((x1 * cos - x2 * sin) * scale).astype(BF16)
            o_ref[:, h * ret_dk + half:(h + 1) * ret_dk] = ((x2 * cos + x1 * sin) * scale).astype(BF16)

    def qk_norm_rope(wn_ref, scale):
        sq = (y * y).astype(BF16)
        c = dc_ref[...]
        s1 = ds1_ref[...]
        s2 = ds2_ref[...]
        rot = diff_dh // 8
        for s in range(d_model // MXU):
            sl = slice(s * MXU, (s + 1) * MXU)
            ss = _dot(sq[:, sl], g_ref[...])
            yn = y[:, sl] * lax.rsqrt(ss * (1.0 / diff_dh) + EPS) * wn_ref[:, sl]
            for t in range(MXU // LANE):
                x = yn[:, t * LANE:(t + 1) * LANE]
                r = x * c + pltpu.roll(x, LANE - rot, 1) * s1 + pltpu.roll(x, rot, 1) * s2
                o_ref[:, s * MXU + t * LANE:s * MXU + (t + 1) * LANE] = (r * scale).astype(BF16)

    def tiles(name):
        ts = [t for t, s in enumerate(seg_of_tile) if s == name]
        cond = j == ts[0]
        for t in ts[1:]:
            cond = cond | (j == t)
        return cond

    @pl.when(tiles("rq"))
    def _():
        ret_rope(1.0)

    @pl.when(tiles("rk"))
    def _():
        ret_rope(ret_k_scale)

    @pl.when(tiles("plain"))
    def _():
        o_ref[...] = y.astype(BF16)

    @pl.when(tiles("silu"))
    def _():
        o_ref[...] = (y * jax.nn.sigmoid(y)).astype(BF16)

    @pl.when(tiles("sigmoid"))
    def _():
        o_ref[...] = jax.nn.sigmoid(y).astype(BF16)

    @pl.when(tiles("dq"))
    def _():
        qk_norm_rope(qw_ref, diff_q_scale)

    @pl.when(tiles("dk"))
    def _():
        qk_norm_rope(kw_ref, 1.0)


def _input_projection(hp, n1, w_in, rcos, rsin, dc, ds1, ds2, qw, kw, gmat, seg_of_tile,
                      lp, ret_dk, diff_dh):
    rows, d_model = hp.shape
    n_col = w_in.shape[1] // d_model
    tm = _pick_tile(lp, (1040, 640, 512, 256, 128))
    tpb = lp // tm
    kern = functools.partial(_proj_kernel, seg_of_tile, d_model, ret_dk, diff_dh,
                             ret_dk ** -0.5, diff_dh ** -0.5)
    tab = pl.BlockSpec((tm, LANE), lambda i, j: (i % tpb, 0))
    vec = pl.BlockSpec((1, d_model), lambda i, j: (0, 0))
    return pl.pallas_call(
        kern,
        grid=(rows // tm, n_col),
        in_specs=[
            pl.BlockSpec((tm, d_model), lambda i, j: (i, 0)),
            vec,
            pl.BlockSpec((d_model, d_model), lambda i, j: (0, j)),
            tab, tab, tab, tab, tab,
            vec, vec,
            pl.BlockSpec((MXU, MXU), lambda i, j: (0, 0)),
        ],
        out_specs=pl.BlockSpec((tm, d_model), lambda i, j: (i, j)),
        out_shape=jax.ShapeDtypeStruct((rows, w_in.shape[1]), BF16),
        scratch_shapes=[pltpu.VMEM((tm, d_model), BF16)],
        compiler_params=pltpu.CompilerParams(
            dimension_semantics=("arbitrary", "arbitrary"), vmem_limit_bytes=VMEM_LIMIT),
        name="in_proj",
    )(hp, n1, w_in, rcos, rsin, dc, ds1, ds2, qw, kw, gmat)


def _retention_kernel(n_heads, dk, dv, q_ref, k_ref, v_ref, g_ref, intra_ref, qdec_ref,
                      kdec_ref, cdec_ref, o_ref, r_ref):
    c = pl.program_id(1)

    @pl.when(c == 0)
    def _():
        r_ref[...] = jnp.zeros_like(r_ref)

    for h in range(n_heads):
        q = q_ref[:, h * dk:(h + 1) * dk]
        k = k_ref[:, h * dk:(h + 1) * dk]
        v = v_ref[:, h * dv:(h + 1) * dv]
        r_old = r_ref[h]
        s = _dot_nt(q, k) * intra_ref[h]
        o = _dot(s.astype(BF16), v) + _dot(q, r_old.astype(BF16)) * qdec_ref[h]
        kd_t = (k.astype(F32) * kdec_ref[h]).T.astype(BF16)
        r_ref[h] = r_old * cdec_ref[h] + _dot(kd_t, v)
        mu = jnp.mean(o, axis=-1, keepdims=True)
        d = o - mu
        var = jnp.mean(d * d, axis=-1, keepdims=True)
        on = d * lax.rsqrt(var + EPS)
        o_ref[:, h * dv:(h + 1) * dv] = (g_ref[:, h * dv:(h + 1) * dv].astype(F32) * on).astype(BF16)


def _retention(proj, batch, lp, d_model, col_q, col_k, col_v, col_g):
    n_heads = RET_HEADS
    dk = d_model // n_heads
    dv = 2 * dk
    n_chunks = lp // CHUNK
    lg = jnp.log(1.0 - jnp.power(2.0, -5.0 - jnp.arange(n_heads, dtype=F32)))
    idx = jnp.arange(CHUNK, dtype=F32)
    dist = idx[:, None] - idx[None, :]
    intra = jnp.where(dist[None] >= 0, jnp.exp(lg[:, None, None] * jnp.maximum(dist, 0.0)[None]), 0.0)
    q_dec = jnp.exp(lg[:, None] * (idx[None] + 1.0))
    k_dec = jnp.exp(lg[:, None] * (CHUNK - 1.0 - idx[None]))
    chunk_dec = jnp.exp(lg * CHUNK)
    qdec = jnp.broadcast_to(q_dec[:, :, None], (n_heads, CHUNK, dv))
    kdec = jnp.broadcast_to(k_dec[:, :, None], (n_heads, CHUNK, dk))
    cdec = jnp.broadcast_to(chunk_dec[:, None, None], (n_heads, dk, dv))

    kern = functools.partial(_retention_kernel, n_heads, dk, dv)
    row = lambda b, c: b * n_chunks + c
    const3 = lambda b, c: (0, 0, 0)
    return pl.pallas_call(
        kern,
        grid=(batch, n_chunks),
        in_specs=[
            pl.BlockSpec((CHUNK, d_model), lambda b, c: (row(b, c), col_q // d_model)),
            pl.BlockSpec((CHUNK, d_model), lambda b, c: (row(b, c), col_k // d_model)),
            pl.BlockSpec((CHUNK, 2 * d_model), lambda b, c: (row(b, c), col_v // (2 * d_model))),
            pl.BlockSpec((CHUNK, 2 * d_model), lambda b, c: (row(b, c), col_g // (2 * d_model))),
            pl.BlockSpec((n_heads, CHUNK, CHUNK), const3),
            pl.BlockSpec((n_heads, CHUNK, dv), const3),
            pl.BlockSpec((n_heads, CHUNK, dk), const3),
            pl.BlockSpec((n_heads, dk, dv), const3),
        ],
        out_specs=pl.BlockSpec((CHUNK, 2 * d_model), lambda b, c: (row(b, c), 0)),
        out_shape=jax.ShapeDtypeStruct((batch * lp, 2 * d_model), BF16),
        scratch_shapes=[pltpu.VMEM((n_heads, dk, dv), F32)],
        compiler_params=pltpu.CompilerParams(
            dimension_semantics=("arbitrary", "arbitrary"), vmem_limit_bytes=VMEM_LIMIT),
        name="retention",
    )(proj, proj, proj, proj, intra, qdec, kdec, cdec)


def _attn_kernel(lp, dh, lam_init, lq1_ref, lk1_ref, lq2_ref, lk2_ref, sw_ref,
                 q_ref, k_ref, v_ref, o_ref):
    tq = ATT_TQ
    n_blocks = (lp - CHUNK) // tq
    lam = (jnp.exp(jnp.sum(lq1_ref[...] * lk1_ref[...], axis=-1, keepdims=True))
           - jnp.exp(jnp.sum(lq2_ref[...] * lk2_ref[...], axis=-1, keepdims=True)) + lam_init)
    sw = sw_ref[...] * (1.0 - lam_init)

    def stack_q(q):
        lo = lax.broadcasted_iota(jnp.int32, q.shape, 1) < dh
        zero = jnp.zeros_like(q)
        return jnp.concatenate([jnp.where(lo, q, zero), jnp.where(lo, zero, q)], axis=0)

    def first_block(qs, kb, vb, mask):
        s = jnp.where(mask, _dot_nt(qs, kb), NEG)
        m = jnp.max(s, axis=-1, keepdims=True)
        p = jnp.exp(s - m)
        l = jnp.sum(p, axis=-1, keepdims=True)
        acc = _dot(p.astype(BF16), vb)
        return m, l, acc

    def next_block(carry, qs, kb, vb, mask):
        m, l, acc = carry
        s = _dot_nt(qs, kb)
        if mask is not None:
            s = jnp.where(mask, s, NEG)
        m_new = jnp.maximum(m, jnp.max(s, axis=-1, keepdims=True))
        alpha = jnp.exp(m - m_new)
        p = jnp.exp(s - m_new)
        l = alpha * l + jnp.sum(p, axis=-1, keepdims=True)
        acc = alpha * acc + _dot(p.astype(BF16), vb)
        return m_new, l, acc

    def finish(carry, n):
        _, l, acc = carry
        o = acc[:n] / l[:n] - lam * (acc[n:] / l[n:])
        ms = jnp.mean(o * o, axis=-1, keepdims=True)
        return (o * lax.rsqrt(ms + EPS) * sw).astype(BF16)

    r0 = lax.broadcasted_iota(jnp.int32, (2 * CHUNK, CHUNK), 0) % CHUNK
    c0 = lax.broadcasted_iota(jnp.int32, (2 * CHUNK, CHUNK), 1)
    k0 = k_ref[0:CHUNK, :]
    v0 = v_ref[0:CHUNK, :]
    carry = first_block(stack_q(q_ref[0:CHUNK, :]), k0, v0, (c0 <= r0) & (c0 >= PAD))
    o_ref[0:CHUNK, :] = finish(carry, CHUNK)

    meta_mask = lax.broadcasted_iota(jnp.int32, (2 * tq, CHUNK), 1) >= PAD
    rd = lax.broadcasted_iota(jnp.int32, (2 * tq, tq), 0) % tq
    cd = lax.broadcasted_iota(jnp.int32, (2 * tq, tq), 1)
    diag_mask = cd <= rd

    def q_block(i, _):
        row = pl.multiple_of(CHUNK + i * tq, CHUNK)
        qs = stack_q(q_ref[pl.ds(row, tq), :])
        carry = first_block(qs, k0, v0, meta_mask)

        def k_block(jb, carry):
            kr = pl.multiple_of(CHUNK + jb * tq, CHUNK)
            return next_block(carry, qs, k_ref[pl.ds(kr, tq), :], v_ref[pl.ds(kr, tq), :], None)

        carry = lax.fori_loop(0, i, k_block, carry)
        carry = next_block(carry, qs, k_ref[pl.ds(row, tq), :], v_ref[pl.ds(row, tq), :], diag_mask)
        o_ref[pl.ds(row, tq), :] = finish(carry, tq)
        return 0

    lax.fori_loop(0, n_blocks, q_block, 0)


def _diff_attention(proj, lq1, lk1, lq2, lk2, subw, batch, lp, d_model, col_q, col_k, col_v,
                    lam_init):
    dv = d_model // DIFF_HEADS
    dh = dv // 2
    proj3 = proj.reshape(batch, lp, proj.shape[1])
    kern = functools.partial(_attn_kernel, lp, dh, lam_init)
    small = lambda n: pl.BlockSpec((1, n), lambda b, h: (0, 0))
    head = lambda col: pl.BlockSpec((None, lp, dv), lambda b, h: (b, 0, col // dv + h))
    return pl.pallas_call(
        kern,
        grid=(batch, DIFF_HEADS),
        in_specs=[small(dh), small(dh), small(dh), small(dh), small(dv),
                  head(col_q), head(col_k), head(col_v)],
        out_specs=pl.BlockSpec((None, lp, dv), lambda b, h: (b, 0, h)),
        out_shape=jax.ShapeDtypeStruct((batch, lp, d_model), BF16),
        compiler_params=pltpu.CompilerParams(
            dimension_semantics=("arbitrary", "arbitrary"), vmem_limit_bytes=VMEM_LIMIT),
        name="diff_attn",
    )(lq1, lk1, lq2, lk2, subw, proj3, proj3, proj3).reshape(batch * lp, d_model)


def _merge_kernel(ro_ref, do_ref, gr_ref, gd_ref, h_ref, wr_ref, wd_ref, wo_ref, n2_ref,
                  h1_ref, u2_ref):
    ro = _dot(ro_ref[...], wr_ref[...])
    do = _dot(do_ref[...], wd_ref[...])
    z = gr_ref[...].astype(F32) * ro + gd_ref[...].astype(F32) * do
    h1 = h_ref[...] + _dot(z.astype(BF16), wo_ref[...])
    h1_ref[...] = h1
    ms = jnp.mean(h1 * h1, axis=-1, keepdims=True)
    u2_ref[...] = (h1 * lax.rsqrt(ms + EPS) * n2_ref[...]).astype(BF16)


def _merge(ro, do, proj, hp, w_ret_o, w_diff_o, w_out, n2, col_gates):
    rows, d_model = hp.shape
    tm = _pick_tile(rows, (640, 512, 256, 128))
    gcol = col_gates // d_model
    const = lambda i: (0, 0)
    return pl.pallas_call(
        _merge_kernel,
        grid=(rows // tm,),
        in_specs=[
            pl.BlockSpec((tm, ro.shape[1]), lambda i: (i, 0)),
            pl.BlockSpec((tm, d_model), lambda i: (i, 0)),
            pl.BlockSpec((tm, d_model), lambda i: (i, gcol)),
            pl.BlockSpec((tm, d_model), lambda i: (i, gcol + 1)),
            pl.BlockSpec((tm, d_model), lambda i: (i, 0)),
            pl.BlockSpec(w_ret_o.shape, const),
            pl.BlockSpec(w_diff_o.shape, const),
            pl.BlockSpec(w_out.shape, const),
            pl.BlockSpec((1, d_model), const),
        ],
        out_specs=[pl.BlockSpec((tm, d_model), lambda i: (i, 0)),
                   pl.BlockSpec((tm, d_model), lambda i: (i, 0))],
        out_shape=[jax.ShapeDtypeStruct((rows, d_model), F32),
                   jax.ShapeDtypeStruct((rows, d_model), BF16)],
        compiler_params=pltpu.CompilerParams(
            dimension_semantics=("arbitrary",), vmem_limit_bytes=VMEM_LIMIT),
        name="merge",
    )(ro, do, proj, proj, hp, w_ret_o, w_diff_o, w_out, n2)


def _ffn_kernel(ffn, tc, u_ref, h_ref, wu_ref, cw_ref, cb_ref, wd_ref, o_ref,
                carry_ref, up_ref, act_ref):
    tm = u_ref.shape[0]
    halo = carry_ref.shape[0]

    @pl.when(pl.program_id(0) == 0)
    def _():
        carry_ref[...] = jnp.zeros_like(carry_ref)

    u = u_ref[...]

    def conv_cols(col):
        up_ref[0:halo, :] = carry_ref[:, col:col + tc]
        up_ref[halo:halo + tm, :] = _dot(u, wu_ref[:, col:col + tc])
        carry_ref[:, col:col + tc] = up_ref[tm:tm + halo, :]
        y = cb_ref[:, col:col + tc]
        for t in range(CONV_W):
            lo = halo - (CONV_W - 1) + t
            y = y + up_ref[lo:lo + tm, :] * cw_ref[t:t + 1, col:col + tc]
        return y

    for c in range(ffn // tc):
        a = conv_cols(c * tc)
        b = conv_cols(ffn + c * tc)
        act_ref[:, c * tc:(c + 1) * tc] = (a * jax.nn.sigmoid(a) * b).astype(BF16)

    o_ref[...] = h_ref[...] + _dot(act_ref[...], wd_ref[...])


def _ffn(u2, h1, w_up, conv_w, conv_b, w_down):
    rows, d_model = h1.shape
    ffn = w_down.shape[0]
    tc = MXU
    halo = 8
    tm = _pick_tile(rows, (640, 512, 256, 128))
    const = lambda i: (0, 0)
    resident = lambda shape: pl.BlockSpec(shape, const, pipeline_mode=pl.Buffered(1))
    kern = functools.partial(_ffn_kernel, ffn, tc)
    return pl.pallas_call(
        kern,
        grid=(rows // tm,),
        in_specs=[
            pl.BlockSpec((tm, d_model), lambda i: (i, 0)),
            pl.BlockSpec((tm, d_model), lambda i: (i, 0)),
            resident(w_up.shape),
            pl.BlockSpec(conv_w.shape, const),
            pl.BlockSpec(conv_b.shape, const),
            resident(w_down.shape),
        ],
        out_specs=pl.BlockSpec((tm, d_model), lambda i: (i, 0)),
        out_shape=jax.ShapeDtypeStruct((rows, d_model), F32),
        scratch_shapes=[pltpu.VMEM((halo, 2 * ffn), F32),
                        pltpu.VMEM((tm + halo, tc), F32),
                        pltpu.VMEM((tm, ffn), BF16)],
        compiler_params=pltpu.CompilerParams(
            dimension_semantics=("arbitrary",), vmem_limit_bytes=VMEM_LIMIT),
        name="conv_ffn",
    )(u2, h1, w_up, conv_w, conv_b, w_down)


def _rope_tables(lp, ret_dk, diff_dh):
    pos = jnp.arange(lp, dtype=F32) - PAD
    half = ret_dk // 2
    inv = jnp.power(RET_THETA, -jnp.arange(half, dtype=F32) / half)
    ang = pos[:, None] * inv[None, :]
    rcos, rsin = jnp.cos(ang), jnp.sin(ang)
    rot = diff_dh // 8
    inv = jnp.power(ROPE_THETA, -jnp.arange(rot, dtype=F32) / rot)
    ang = pos[:, None] * inv[None, :]
    cos, sin = jnp.cos(ang), jnp.sin(ang)
    ones = jnp.ones((lp, diff_dh - 2 * rot), F32)
    zeros = jnp.zeros((lp, diff_dh - 2 * rot), F32)
    zr = jnp.zeros((lp, rot), F32)
    rep = LANE // diff_dh
    dc = jnp.tile(jnp.concatenate([cos, cos, ones], axis=1), (1, rep))
    ds1 = jnp.tile(jnp.concatenate([-sin, zr, zeros], axis=1), (1, rep))
    ds2 = jnp.tile(jnp.concatenate([zr, sin, zeros], axis=1), (1, rep))
    return rcos, rsin, dc, ds1, ds2


def kernel(x, meta_tokens, norm1_w, w_in, w_ret_o, q_norm_w, k_norm_w, lambda_q1, lambda_k1,
           lambda_q2, lambda_k2, diff_subln_w, w_diff_o, w_out, norm2_w, w_up, conv_w, conv_b,
           w_down):
    batch, seq, d_model = x.shape
    depth = norm1_w.shape[0]
    lp = CHUNK + seq
    ret_dk = d_model // RET_HEADS
    diff_dv = d_model // DIFF_HEADS
    diff_dh = diff_dv // 2
    assert (lp - CHUNK) % ATT_TQ == 0 and d_model % MXU == 0 and ret_dk // 2 == LANE

    seg_of_tile = ("rq", "rk", "plain", "plain", "silu", "silu", "dq", "dk", "plain",
                   "sigmoid", "sigmoid")
    col_rq, col_rk, col_rv, col_rg = 0, d_model, 2 * d_model, 4 * d_model
    col_dq, col_dk, col_dv, col_gates = 6 * d_model, 7 * d_model, 8 * d_model, 9 * d_model
    assert w_in.shape[2] == len(seg_of_tile) * d_model

    rcos, rsin, dc, ds1, ds2 = _rope_tables(lp, ret_dk, diff_dh)
    gid = np.arange(MXU) // diff_dh
    gmat = jnp.asarray(gid[:, None] == gid[None, :], dtype=BF16)

    dt = x.dtype
    hp = jnp.concatenate([jnp.zeros((batch, PAD, d_model), dt),
                          jnp.broadcast_to(meta_tokens.astype(dt)[None], (batch, N_META, d_model)),
                          x], axis=1).reshape(batch * lp, d_model)

    for l in range(depth):
        lam_init = 0.8 - 0.6 * math.exp(-0.3 * l)
        row = lambda a: a[l].astype(F32).reshape(1, -1)
        rep = d_model // diff_dh
        proj = _input_projection(
            hp, row(norm1_w), w_in[l].astype(BF16), rcos, rsin, dc, ds1, ds2,
            jnp.tile(row(q_norm_w), (1, rep)), jnp.tile(row(k_norm_w), (1, rep)), gmat,
            seg_of_tile, lp, ret_dk, diff_dh)
        ro = _retention(proj, batch, lp, d_model, col_rq, col_rk, col_rv, col_rg)
        do = _diff_attention(proj, row(lambda_q1), row(lambda_k1), row(lambda_q2), row(lambda_k2),
                             row(diff_subln_w), batch, lp, d_model, col_dq, col_dk, col_dv, lam_init)
        h1, u2 = _merge(ro, do, proj, hp, w_ret_o[l].astype(BF16), w_diff_o[l].astype(BF16),
                        w_out[l].astype(BF16), row(norm2_w), col_gates)
        hp = _ffn(u2, h1, w_up[l].astype(BF16), conv_w[l].astype(F32),
                  conv_b[l].astype(F32).reshape(1, -1), w_down[l].astype(BF16))

    return hp.reshape(batch, lp, d_model)[:, CHUNK:]
```

```python
import functools
import math

import numpy as np
import jax
import jax.numpy as jnp
from jax import lax
from jax.experimental import pallas as pl
from jax.experimental.pallas import tpu as pltpu

F32 = jnp.float32
BF16 = jnp.bfloat16

N_META = 16
CHUNK = 128
PAD = CHUNK - N_META
RET_HEADS = 4
RET_THETA = 10000.0
DIFF_HEADS = 8
ROPE_THETA = 500000.0
CONV_W = 3
EPS = 1e-6
NEG = -1e30
LANE = 128
MXU = 256
VMEM_LIMIT = 56 * 1024 * 1024

ATT_TQ = 256
ATT_GROUP_SHIFT = 2
LOG2E = math.log2(math.e)


def _pick_tile(n, candidates):
    for c in candidates:
        if n % c == 0:
            return c
    raise ValueError(f"no tile for {n}")


def _dot(a, b):
    return jnp.dot(a, b, preferred_element_type=F32)


def _dot_nt(a, b):
    return lax.dot_general(a, b, (((1,), (1,)), ((), ())), preferred_element_type=F32)


def _sigmoid(v):
    return 0.5 * jnp.tanh(0.5 * v) + 0.5


def _proj_kernel(seg_of_tile, d_model, ret_dk, diff_dh, ret_k_scale, diff_q_scale,
                 x_ref, n1_ref, w_ref, rcos_ref, rsin_ref, dc_ref, ds1_ref, ds2_ref,
                 qw_ref, kw_ref, g_ref, o_ref, u_ref):
    j = pl.program_id(1)

    @pl.when(j == 0)
    def _():
        x = x_ref[...]
        ms = jnp.mean(x * x, axis=-1, keepdims=True)
        u_ref[...] = (x * lax.rsqrt(ms + EPS) * n1_ref[...]).astype(BF16)

    half = ret_dk // 2

    def project():
        return _dot(u_ref[...], w_ref[...])

    def ret_rope(scale):
        y = project()
        cos = rcos_ref[...]
        sin = rsin_ref[...]
        for h in range(d_model // ret_dk):
            x1 = y[:, h * ret_dk:h * ret_dk + half]
            x2 = y[:, h * ret_dk + half:(h + 1) * ret_dk]
            o_ref[:, h * ret_dk:h * ret_dk + half] = ((x1 * cos - x2 * sin) * scale).astype(BF16)
            o_ref[:, h * ret_dk + half:(h + 1) * ret_dk] = ((x2 * cos + x1 * sin) * scale).astype(BF16)

    def qk_norm_rope(wn_ref, scale):
        y = project()
        sq = (y * y).astype(BF16)
        c = dc_ref[...]
        s1 = ds1_ref[...]
        s2 = ds2_ref[...]
        rot = diff_dh // 8
        for s in range(d_model // MXU):
            sl = slice(s * MXU, (s + 1) * MXU)
            ss = _dot(sq[:, sl], g_ref[...])
            yn = y[:, sl] * lax.rsqrt(ss * (1.0 / diff_dh) + EPS) * wn_ref[:, sl]
            for t in range(MXU // LANE):
                x = yn[:, t * LANE:(t + 1) * LANE]
                r = x * c + pltpu.roll(x, LANE - rot, 1) * s1 + pltpu.roll(x, rot, 1) * s2
                o_ref[:, s * MXU + t * LANE:s * MXU + (t + 1) * LANE] = (r * scale).astype(BF16)

    def tiles(name):
        ts = [t for t, s in enumerate(seg_of_tile) if s == name]
        cond = j == ts[0]
        for t in ts[1:]:
            cond = cond | (j == t)
        return cond

    @pl.when(tiles("rq"))
    def _():
        ret_rope(1.0)

    @pl.when(tiles("rk"))
    def _():
        ret_rope(ret_k_scale)

    @pl.when(tiles("plain"))
    def _():
        o_ref[...] = project().astype(BF16)

    @pl.when(tiles("silu"))
    def _():
        y = project()
        o_ref[...] = (y * _sigmoid(y)).astype(BF16)

    @pl.when(tiles("sigmoid"))
    def _():
        o_ref[...] = _sigmoid(project()).astype(BF16)

    @pl.when(tiles("dq"))
    def _():
        qk_norm_rope(qw_ref, diff_q_scale)

    @pl.when(tiles("dk"))
    def _():
        qk_norm_rope(kw_ref, 1.0)


def _input_projection(hp, n1, w_in, rcos, rsin, dc, ds1, ds2, qw, kw, gmat, seg_of_tile,
                      lp, ret_dk, diff_dh):
    rows, d_model = hp.shape
    n_col = w_in.shape[1] // d_model
    tm = _pick_tile(lp, (1040, 640, 512, 256, 128))
    tpb = lp // tm
    kern = functools.partial(_proj_kernel, seg_of_tile, d_model, ret_dk, diff_dh,
                             ret_dk ** -0.5, diff_dh ** -0.5 * LOG2E)
    tab = pl.BlockSpec((tm, LANE), lambda i, j: (i % tpb, 0))
    vec = pl.BlockSpec((1, d_model), lambda i, j: (0, 0))
    return pl.pallas_call(
        kern,
        grid=(rows // tm, n_col),
        in_specs=[
            pl.BlockSpec((tm, d_model), lambda i, j: (i, 0)),
            vec,
            pl.BlockSpec((d_model, d_model), lambda i, j: (0, j)),
            tab, tab, tab, tab, tab,
            vec, vec,
            pl.BlockSpec((MXU, MXU), lambda i, j: (0, 0)),
        ],
        out_specs=pl.BlockSpec((tm, d_model), lambda i, j: (i, j)),
        out_shape=jax.ShapeDtypeStruct((rows, w_in.shape[1]), BF16),
        scratch_shapes=[pltpu.VMEM((tm, d_model), BF16)],
        compiler_params=pltpu.CompilerParams(
            dimension_semantics=("arbitrary", "arbitrary"), vmem_limit_bytes=VMEM_LIMIT),
        name="in_proj",
    )(hp, n1, w_in, rcos, rsin, dc, ds1, ds2, qw, kw, gmat)


def _retention_kernel(n_heads, dk, dv, q_ref, k_ref, v_ref, g_ref, intra_ref, qdec_ref,
                      kdec_ref, cdec_ref, o_ref, r_ref):
    c = pl.program_id(1)

    @pl.when(c == 0)
    def _():
        r_ref[...] = jnp.zeros_like(r_ref)

    for h in range(n_heads):
        q = q_ref[:, h * dk:(h + 1) * dk]
        k = k_ref[:, h * dk:(h + 1) * dk]
        v = v_ref[:, h * dv:(h + 1) * dv]
        r_old = r_ref[h]
        s = _dot_nt(q, k) * intra_ref[h]
        o = _dot(s.astype(BF16), v) + _dot(q, r_old.astype(BF16)) * qdec_ref[h]
        kd_t = (k.astype(F32) * kdec_ref[h]).T.astype(BF16)
        r_ref[h] = r_old * cdec_ref[h] + _dot(kd_t, v)
        mu = jnp.mean(o, axis=-1, keepdims=True)
        d = o - mu
        var = jnp.mean(d * d, axis=-1, keepdims=True)
        on = d * lax.rsqrt(var + EPS)
        o_ref[:, h * dv:(h + 1) * dv] = (g_ref[:, h * dv:(h + 1) * dv].astype(F32) * on).astype(BF16)


def _retention(proj, batch, lp, d_model, col_q, col_k, col_v, col_g):
    n_heads = RET_HEADS
    dk = d_model // n_heads
    dv = 2 * dk
    n_chunks = lp // CHUNK
    lg = jnp.log(1.0 - jnp.power(2.0, -5.0 - jnp.arange(n_heads, dtype=F32)))
    idx = jnp.arange(CHUNK, dtype=F32)
    dist = idx[:, None] - idx[None, :]
    intra = jnp.where(dist[None] >= 0, jnp.exp(lg[:, None, None] * jnp.maximum(dist, 0.0)[None]), 0.0)
    q_dec = jnp.exp(lg[:, None] * (idx[None] + 1.0))
    k_dec = jnp.exp(lg[:, None] * (CHUNK - 1.0 - idx[None]))
    chunk_dec = jnp.exp(lg * CHUNK)
    qdec = jnp.broadcast_to(q_dec[:, :, None], (n_heads, CHUNK, dv))
    kdec = jnp.broadcast_to(k_dec[:, :, None], (n_heads, CHUNK, dk))
    cdec = jnp.broadcast_to(chunk_dec[:, None, None], (n_heads, dk, dv))

    kern = functools.partial(_retention_kernel, n_heads, dk, dv)
    row = lambda b, c: b * n_chunks + c
    const3 = lambda b, c: (0, 0, 0)
    return pl.pallas_call(
        kern,
        grid=(batch, n_chunks),
        in_specs=[
            pl.BlockSpec((CHUNK, d_model), lambda b, c: (row(b, c), col_q // d_model)),
            pl.BlockSpec((CHUNK, d_model), lambda b, c: (row(b, c), col_k // d_model)),
            pl.BlockSpec((CHUNK, 2 * d_model), lambda b, c: (row(b, c), col_v // (2 * d_model))),
            pl.BlockSpec((CHUNK, 2 * d_model), lambda b, c: (row(b, c), col_g // (2 * d_model))),
            pl.BlockSpec((n_heads, CHUNK, CHUNK), const3),
            pl.BlockSpec((n_heads, CHUNK, dv), const3),
            pl.BlockSpec((n_heads, CHUNK, dk), const3),
            pl.BlockSpec((n_heads, dk, dv), const3),
        ],
        out_specs=pl.BlockSpec((CHUNK, 2 * d_model), lambda b, c: (row(b, c), 0)),
        out_shape=jax.ShapeDtypeStruct((batch * lp, 2 * d_model), BF16),
        scratch_shapes=[pltpu.VMEM((n_heads, dk, dv), F32)],
        compiler_params=pltpu.CompilerParams(
            dimension_semantics=("arbitrary", "arbitrary"), vmem_limit_bytes=VMEM_LIMIT),
        name="retention",
    )(proj, proj, proj, proj, intra, qdec, kdec, cdec)


def _lambda(lq1_ref, lk1_ref, lq2_ref, lk2_ref, lam_init):
    return (jnp.exp(jnp.sum(lq1_ref[...] * lk1_ref[...], axis=-1, keepdims=True))
            - jnp.exp(jnp.sum(lq2_ref[...] * lk2_ref[...], axis=-1, keepdims=True)) + lam_init)


def _stack_q(q, dh):
    lo = lax.broadcasted_iota(jnp.int32, q.shape, 1) < dh
    zero = jnp.zeros_like(q)
    return jnp.concatenate([jnp.where(lo, q, zero), jnp.where(lo, zero, q)], axis=0)


def _attn_bounded_kernel(lp, dh, lam_init, bound_ref, lq1_ref, lk1_ref, lq2_ref, lk2_ref, sw_ref,
                         q_ref, k_ref, v_ref, o_ref, v1_ref, acc_ref):
    tq = ATT_TQ
    dv = 2 * dh
    n_blocks = (lp - CHUNK) // tq
    bound = bound_ref[0]
    lam = _lambda(lq1_ref, lk1_ref, lq2_ref, lk2_ref, lam_init)
    sw = sw_ref[...] * (1.0 - lam_init)

    v1_ref[:, 0:dv] = v_ref[...]
    v1_ref[:, dv:2 * dv] = jnp.ones((lp, dv), BF16)

    def probs(s, mask):
        if mask is not None:
            s = jnp.where(mask, s, NEG)
        return jnp.exp2(s - bound).astype(BF16)

    def finish(acc, n):
        r = acc[:, :dv] / jnp.maximum(acc[:, dv:], 1e-30)
        o = r[:n] - lam * r[n:]
        ms = jnp.mean(o * o, axis=-1, keepdims=True)
        return (o * lax.rsqrt(ms + EPS) * sw).astype(BF16)

    r0 = lax.broadcasted_iota(jnp.int32, (2 * CHUNK, CHUNK), 0) % CHUNK
    c0 = lax.broadcasted_iota(jnp.int32, (2 * CHUNK, CHUNK), 1)
    k0 = k_ref[0:CHUNK, :]
    v0 = v1_ref[0:CHUNK, :]
    p = probs(_dot_nt(_stack_q(q_ref[0:CHUNK, :], dh), k0), (c0 <= r0) & (c0 >= PAD))
    o_ref[0:CHUNK, :] = finish(_dot(p, v0), CHUNK)

    meta_mask = lax.broadcasted_iota(jnp.int32, (2 * tq, CHUNK), 1) >= PAD
    rd = lax.broadcasted_iota(jnp.int32, (2 * tq, tq), 0) % tq
    cd = lax.broadcasted_iota(jnp.int32, (2 * tq, tq), 1)
    diag_mask = cd <= rd

    def q_block(i, _):
        row = pl.multiple_of(CHUNK + i * tq, CHUNK)
        qs = _stack_q(q_ref[pl.ds(row, tq), :], dh)
        acc_ref[...] = _dot(probs(_dot_nt(qs, k0), meta_mask), v0)

        def key_blocks(kr, n, mask):
            ps = [probs(_dot_nt(qs, k_ref[pl.ds(kr + g * tq, tq), :]), mask) for g in range(n)]
            p = ps[0] if n == 1 else jnp.concatenate(ps, axis=1)
            acc_ref[...] += _dot(p, v1_ref[pl.ds(kr, n * tq), :])

        n_groups = lax.shift_right_logical(i, ATT_GROUP_SHIFT)
        group = 1 << ATT_GROUP_SHIFT

        def grouped(t, _):
            key_blocks(pl.multiple_of(CHUNK + t * (group * tq), CHUNK), group, None)
            return 0

        def single(t, _):
            key_blocks(pl.multiple_of(CHUNK + (n_groups * group + t) * tq, CHUNK), 1, None)
            return 0

        lax.fori_loop(0, n_groups, grouped, 0)
        lax.fori_loop(0, i & (group - 1), single, 0)
        key_blocks(row, 1, diag_mask)
        o_ref[pl.ds(row, tq), :] = finish(acc_ref[...], tq)
        return 0

    lax.fori_loop(0, n_blocks, q_block, 0)


def _attn_online_kernel(lp, dh, lam_init, lq1_ref, lk1_ref, lq2_ref, lk2_ref, sw_ref,
                        q_ref, k_ref, v_ref, o_ref):
    tq = ATT_TQ
    n_blocks = (lp - CHUNK) // tq
    lam = _lambda(lq1_ref, lk1_ref, lq2_ref, lk2_ref, lam_init)
    sw = sw_ref[...] * (1.0 - lam_init)
    stack_q = functools.partial(_stack_q, dh=dh)

    def first_block(qs, kb, vb, mask):
        s = jnp.where(mask, _dot_nt(qs, kb), NEG)
        m = jnp.max(s, axis=-1, keepdims=True)
        p = jnp.exp2(s - m)
        l = jnp.sum(p, axis=-1, keepdims=True)
        acc = _dot(p.astype(BF16), vb)
        return m, l, acc

    def next_block(carry, qs, kb, vb, mask):
        m, l, acc = carry
        s = _dot_nt(qs, kb)
        if mask is not None:
            s = jnp.where(mask, s, NEG)
        m_new = jnp.maximum(m, jnp.max(s, axis=-1, keepdims=True))
        alpha = jnp.exp2(m - m_new)
        p = jnp.exp2(s - m_new)
        l = alpha * l + jnp.sum(p, axis=-1, keepdims=True)
        acc = alpha * acc + _dot(p.astype(BF16), vb)
        return m_new, l, acc

    def finish(carry, n):
        _, l, acc = carry
        o = acc[:n] / l[:n] - lam * (acc[n:] / l[n:])
        ms = jnp.mean(o * o, axis=-1, keepdims=True)
        return (o * lax.rsqrt(ms + EPS) * sw).astype(BF16)

    r0 = lax.broadcasted_iota(jnp.int32, (2 * CHUNK, CHUNK), 0) % CHUNK
    c0 = lax.broadcasted_iota(jnp.int32, (2 * CHUNK, CHUNK), 1)
    k0 = k_ref[0:CHUNK, :]
    v0 = v_ref[0:CHUNK, :]
    carry = first_block(stack_q(q_ref[0:CHUNK, :]), k0, v0, (c0 <= r0) & (c0 >= PAD))
    o_ref[0:CHUNK, :] = finish(carry, CHUNK)

    meta_mask = lax.broadcasted_iota(jnp.int32, (2 * tq, CHUNK), 1) >= PAD
    rd = lax.broadcasted_iota(jnp.int32, (2 * tq, tq), 0) % tq
    cd = lax.broadcasted_iota(jnp.int32, (2 * tq, tq), 1)
    diag_mask = cd <= rd

    def q_block(i, _):
        row = pl.multiple_of(CHUNK + i * tq, CHUNK)
        qs = stack_q(q_ref[pl.ds(row, tq), :])
        carry = first_block(qs, k0, v0, meta_mask)

        def k_block(jb, carry):
            kr = pl.multiple_of(CHUNK + jb * tq, CHUNK)
            return next_block(carry, qs, k_ref[pl.ds(kr, tq), :], v_ref[pl.ds(kr, tq), :], None)

        carry = lax.fori_loop(0, i, k_block, carry)
        carry = next_block(carry, qs, k_ref[pl.ds(row, tq), :], v_ref[pl.ds(row, tq), :], diag_mask)
        o_ref[pl.ds(row, tq), :] = finish(carry, tq)
        return 0

    lax.fori_loop(0, n_blocks, q_block, 0)


MAX_SAFE_SCORE_BOUND = 60.0


def _diff_attention(proj, score_bound, lq1, lk1, lq2, lk2, subw, batch, lp, d_model,
                    col_q, col_k, col_v, lam_init):
    dv = d_model // DIFF_HEADS
    dh = dv // 2
    proj3 = proj.reshape(batch, lp, proj.shape[1])
    small = lambda n: pl.BlockSpec((1, n), lambda b, h: (0, 0))
    head = lambda col: pl.BlockSpec((None, lp, dv), lambda b, h: (b, 0, col // dv + h))
    common = dict(
        grid=(batch, DIFF_HEADS),
        out_specs=pl.BlockSpec((None, lp, dv), lambda b, h: (b, 0, h)),
        out_shape=jax.ShapeDtypeStruct((batch, lp, d_model), BF16),
        compiler_params=pltpu.CompilerParams(
            dimension_semantics=("arbitrary", "arbitrary"), vmem_limit_bytes=VMEM_LIMIT),
    )
    tensors = [small(dh), small(dh), small(dh), small(dh), small(dv),
               head(col_q), head(col_k), head(col_v)]

    def bounded():
        return pl.pallas_call(
            functools.partial(_attn_bounded_kernel, lp, dh, lam_init),
            in_specs=[pl.BlockSpec(memory_space=pltpu.SMEM)] + tensors,
            scratch_shapes=[pltpu.VMEM((lp, 2 * dv), BF16),
                            pltpu.VMEM((2 * ATT_TQ, 2 * dv), F32)],
            name="diff_attn", **common,
        )(score_bound, lq1, lk1, lq2, lk2, subw, proj3, proj3, proj3)

    def online():
        return pl.pallas_call(
            functools.partial(_attn_online_kernel, lp, dh, lam_init),
            in_specs=tensors, name="diff_attn_online", **common,
        )(lq1, lk1, lq2, lk2, subw, proj3, proj3, proj3)

    out = lax.cond(score_bound[0] <= MAX_SAFE_SCORE_BOUND, bounded, online)
    return out.reshape(batch * lp, d_model)


def _merge_kernel(ro_ref, do_ref, gr_ref, gd_ref, h_ref, wr_ref, wd_ref, wo_ref, n2_ref,
                  h1_ref, u2_ref):
    ro = _dot(ro_ref[...], wr_ref[...])
    do = _dot(do_ref[...], wd_ref[...])
    z = gr_ref[...].astype(F32) * ro + gd_ref[...].astype(F32) * do
    h1 = h_ref[...] + _dot(z.astype(BF16), wo_ref[...])
    h1_ref[...] = h1
    ms = jnp.mean(h1 * h1, axis=-1, keepdims=True)
    u2_ref[...] = (h1 * lax.rsqrt(ms + EPS) * n2_ref[...]).astype(BF16)


def _merge(ro, do, proj, hp, w_ret_o, w_diff_o, w_out, n2, col_gates):
    rows, d_model = hp.shape
    tm = _pick_tile(rows, (640, 512, 256, 128))
    gcol = col_gates // d_model
    const = lambda i: (0, 0)
    return pl.pallas_call(
        _merge_kernel,
        grid=(rows // tm,),
        in_specs=[
            pl.BlockSpec((tm, ro.shape[1]), lambda i: (i, 0)),
            pl.BlockSpec((tm, d_model), lambda i: (i, 0)),
            pl.BlockSpec((tm, d_model), lambda i: (i, gcol)),
            pl.BlockSpec((tm, d_model), lambda i: (i, gcol + 1)),
            pl.BlockSpec((tm, d_model), lambda i: (i, 0)),
            pl.BlockSpec(w_ret_o.shape, const),
            pl.BlockSpec(w_diff_o.shape, const),
            pl.BlockSpec(w_out.shape, const),
            pl.BlockSpec((1, d_model), const),
        ],
        out_specs=[pl.BlockSpec((tm, d_model), lambda i: (i, 0)),
                   pl.BlockSpec((tm, d_model), lambda i: (i, 0))],
        out_shape=[jax.ShapeDtypeStruct((rows, d_model), F32),
                   jax.ShapeDtypeStruct((rows, d_model), BF16)],
        compiler_params=pltpu.CompilerParams(
            dimension_semantics=("arbitrary",), vmem_limit_bytes=VMEM_LIMIT),
        name="merge",
    )(ro, do, proj, proj, hp, w_ret_o, w_diff_o, w_out, n2)


def _ffn_kernel(ffn, tc, u_ref, h_ref, wu_ref, cw_ref, cb_ref, wd_ref, o_ref,
                carry_ref, up_ref, act_ref):
    tm = u_ref.shape[0]
    halo = carry_ref.shape[0]

    @pl.when(pl.program_id(0) == 0)
    def _():
        carry_ref[...] = jnp.zeros_like(carry_ref)

    u = u_ref[...]

    def conv_cols(col):
        up_ref[0:halo, :] = carry_ref[:, col:col + tc]
        up_ref[halo:halo + tm, :] = _dot(u, wu_ref[:, col:col + tc])
        carry_ref[:, col:col + tc] = up_ref[tm:tm + halo, :]
        y = cb_ref[:, col:col + tc]
        for t in range(CONV_W):
            lo = halo - (CONV_W - 1) + t
            y = y + up_ref[lo:lo + tm, :] * cw_ref[t:t + 1, col:col + tc]
        return y

    for c in range(ffn // tc):
        a = conv_cols(c * tc)
        b = conv_cols(ffn + c * tc)
        act_ref[:, c * tc:(c + 1) * tc] = (a * _sigmoid(a) * b).astype(BF16)

    o_ref[...] = h_ref[...] + _dot(act_ref[...], wd_ref[...])


def _ffn(u2, h1, w_up, conv_w, conv_b, w_down):
    rows, d_model = h1.shape
    ffn = w_down.shape[0]
    tc = MXU
    halo = 8
    tm = _pick_tile(rows, (640, 512, 256, 128))
    const = lambda i: (0, 0)
    resident = lambda shape: pl.BlockSpec(shape, const, pipeline_mode=pl.Buffered(1))
    kern = functools.partial(_ffn_kernel, ffn, tc)
    return pl.pallas_call(
        kern,
        grid=(rows // tm,),
        in_specs=[
            pl.BlockSpec((tm, d_model), lambda i: (i, 0)),
            pl.BlockSpec((tm, d_model), lambda i: (i, 0)),
            resident(w_up.shape),
            pl.BlockSpec(conv_w.shape, const),
            pl.BlockSpec(conv_b.shape, const),
            resident(w_down.shape),
        ],
        out_specs=pl.BlockSpec((tm, d_model), lambda i: (i, 0)),
        out_shape=jax.ShapeDtypeStruct((rows, d_model), F32),
        scratch_shapes=[pltpu.VMEM((halo, 2 * ffn), F32),
                        pltpu.VMEM((tm + halo, tc), F32),
                        pltpu.VMEM((tm, ffn), BF16)],
        compiler_params=pltpu.CompilerParams(
            dimension_semantics=("arbitrary",), vmem_limit_bytes=VMEM_LIMIT),
        name="conv_ffn",
    )(u2, h1, w_up, conv_w, conv_b, w_down)


def _rope_tables(lp, ret_dk, diff_dh):
    pos = jnp.arange(lp, dtype=F32) - PAD
    half = ret_dk // 2
    inv = jnp.power(RET_THETA, -jnp.arange(half, dtype=F32) / half)
    ang = pos[:, None] * inv[None, :]
    rcos, rsin = jnp.cos(ang), jnp.sin(ang)
    rot = diff_dh // 8
    inv = jnp.power(ROPE_THETA, -jnp.arange(rot, dtype=F32) / rot)
    ang = pos[:, None] * inv[None, :]
    cos, sin = jnp.cos(ang), jnp.sin(ang)
    ones = jnp.ones((lp, diff_dh - 2 * rot), F32)
    zeros = jnp.zeros((lp, diff_dh - 2 * rot), F32)
    zr = jnp.zeros((lp, rot), F32)
    rep = LANE // diff_dh
    dc = jnp.tile(jnp.concatenate([cos, cos, ones], axis=1), (1, rep))
    ds1 = jnp.tile(jnp.concatenate([-sin, zr, zeros], axis=1), (1, rep))
    ds2 = jnp.tile(jnp.concatenate([zr, sin, zeros], axis=1), (1, rep))
    return rcos, rsin, dc, ds1, ds2


def kernel(x, meta_tokens, norm1_w, w_in, w_ret_o, q_norm_w, k_norm_w, lambda_q1, lambda_k1,
           lambda_q2, lambda_k2, diff_subln_w, w_diff_o, w_out, norm2_w, w_up, conv_w, conv_b,
           w_down):
    batch, seq, d_model = x.shape
    depth = norm1_w.shape[0]
    lp = CHUNK + seq
    ret_dk = d_model // RET_HEADS
    diff_dv = d_model // DIFF_HEADS
    diff_dh = diff_dv // 2
    assert (lp - CHUNK) % ATT_TQ == 0 and d_model % MXU == 0 and ret_dk // 2 == LANE

    seg_of_tile = ("rq", "rk", "plain", "plain", "silu", "silu", "dq", "dk", "plain",
                   "sigmoid", "sigmoid")
    col_rq, col_rk, col_rv, col_rg = 0, d_model, 2 * d_model, 4 * d_model
    col_dq, col_dk, col_dv, col_gates = 6 * d_model, 7 * d_model, 8 * d_model, 9 * d_model
    assert w_in.shape[2] == len(seg_of_tile) * d_model

    rcos, rsin, dc, ds1, ds2 = _rope_tables(lp, ret_dk, diff_dh)
    gid = np.arange(MXU) // diff_dh
    gmat = jnp.asarray(gid[:, None] == gid[None, :], dtype=BF16)

    dt = x.dtype
    hp = jnp.concatenate([jnp.zeros((batch, PAD, d_model), dt),
                          jnp.broadcast_to(meta_tokens.astype(dt)[None], (batch, N_META, d_model)),
                          x], axis=1).reshape(batch * lp, d_model)

    for l in range(depth):
        lam_init = 0.8 - 0.6 * math.exp(-0.3 * l)
        row = lambda a: a[l].astype(F32).reshape(1, -1)
        rep = d_model // diff_dh
        proj = _input_projection(
            hp, row(norm1_w), w_in[l].astype(BF16), rcos, rsin, dc, ds1, ds2,
            jnp.tile(row(q_norm_w), (1, rep)), jnp.tile(row(k_norm_w), (1, rep)), gmat,
            seg_of_tile, lp, ret_dk, diff_dh)
        ro = _retention(proj, batch, lp, d_model, col_rq, col_rk, col_rv, col_rg)
        score_bound = (jnp.max(jnp.abs(q_norm_w[l].astype(F32))) * jnp.max(jnp.abs(k_norm_w[l].astype(F32)))
                       * (diff_dh ** 0.5 * LOG2E * (1.0 + 2.0 ** -5))).reshape(1)
        do = _diff_attention(proj, score_bound, row(lambda_q1), row(lambda_k1), row(lambda_q2),
                             row(lambda_k2), row(diff_subln_w), batch, lp, d_model,
                             col_dq, col_dk, col_dv, lam_init)
        h1, u2 = _merge(ro, do, proj, hp, w_ret_o[l].astype(BF16), w_diff_o[l].astype(BF16),
                        w_out[l].astype(BF16), row(norm2_w), col_gates)
        hp = _ffn(u2, h1, w_up[l].astype(BF16), conv_w[l].astype(F32),
                  conv_b[l].astype(F32).reshape(1, -1), w_down[l].astype(BF16))

    return hp.reshape(batch, lp, d_model)[:, CHUNK:]
```

```python
import functools
import math

import numpy as np
import jax
import jax.numpy as jnp
from jax import lax
from jax.experimental import pallas as pl
from jax.experimental.pallas import tpu as pltpu

F32 = jnp.float32
BF16 = jnp.bfloat16

N_META = 16
CHUNK = 128
PAD = CHUNK - N_META
RET_HEADS = 4
RET_THETA = 10000.0
DIFF_HEADS = 8
ROPE_THETA = 500000.0
CONV_W = 3
EPS = 1e-6
NEG = -1e30
LANE = 128
MXU = 256
VMEM_LIMIT = 56 * 1024 * 1024

ATT_TQ = 256
ATT_T = 512
ATT_GROUP = 4
LOG2E = math.log2(math.e)


def _pick_tile(n, candidates):
    for c in candidates:
        if n % c == 0:
            return c
    raise ValueError(f"no tile for {n}")


def _dot(a, b):
    return jnp.dot(a, b, preferred_element_type=F32)


def _dot_nt(a, b):
    return lax.dot_general(a, b, (((1,), (1,)), ((), ())), preferred_element_type=F32)


def _sigmoid(v):
    return 0.5 * jnp.tanh(0.5 * v) + 0.5


def _proj_kernel(seg_of_tile, d_model, ret_dk, diff_dh, ret_k_scale, diff_q_scale,
                 x_ref, n1_ref, w_ref, rcos_ref, rsin_ref, dc_ref, ds1_ref, ds2_ref,
                 qw_ref, kw_ref, g_ref, o_ref, u_ref):
    j = pl.program_id(1)

    @pl.when(j == 0)
    def _():
        x = x_ref[...]
        ms = jnp.mean(x * x, axis=-1, keepdims=True)
        u_ref[...] = (x * lax.rsqrt(ms + EPS) * n1_ref[...]).astype(BF16)

    half = ret_dk // 2

    def project():
        return _dot(u_ref[...], w_ref[...])

    def ret_rope(scale):
        y = project()
        cos = rcos_ref[...]
        sin = rsin_ref[...]
        for h in range(d_model // ret_dk):
            x1 = y[:, h * ret_dk:h * ret_dk + half]
            x2 = y[:, h * ret_dk + half:(h + 1) * ret_dk]
            o_ref[:, h * ret_dk:h * ret_dk + half] = ((x1 * cos - x2 * sin) * scale).astype(BF16)
            o_ref[:, h * ret_dk + half:(h + 1) * ret_dk] = ((x2 * cos + x1 * sin) * scale).astype(BF16)

    def qk_norm_rope(wn_ref, scale):
        y = project()
        sq = (y * y).astype(BF16)
        c = dc_ref[...]
        s1 = ds1_ref[...]
        s2 = ds2_ref[...]
        rot = diff_dh // 8
        for s in range(d_model // MXU):
            sl = slice(s * MXU, (s + 1) * MXU)
            ss = _dot(sq[:, sl], g_ref[...])
            yn = y[:, sl] * lax.rsqrt(ss * (1.0 / diff_dh) + EPS) * wn_ref[:, sl]
            for t in range(MXU // LANE):
                x = yn[:, t * LANE:(t + 1) * LANE]
                r = x * c + pltpu.roll(x, LANE - rot, 1) * s1 + pltpu.roll(x, rot, 1) * s2
                o_ref[:, s * MXU + t * LANE:s * MXU + (t + 1) * LANE] = (r * scale).astype(BF16)

    def tiles(name):
        ts = [t for t, s in enumerate(seg_of_tile) if s == name]
        cond = j == ts[0]
        for t in ts[1:]:
            cond = cond | (j == t)
        return cond

    @pl.when(tiles("rq"))
    def _():
        ret_rope(1.0)

    @pl.when(tiles("rk"))
    def _():
        ret_rope(ret_k_scale)

    @pl.when(tiles("plain"))
    def _():
        o_ref[...] = project().astype(BF16)

    @pl.when(tiles("silu"))
    def _():
        y = project()
        o_ref[...] = (y * _sigmoid(y)).astype(BF16)

    @pl.when(tiles("sigmoid"))
    def _():
        o_ref[...] = _sigmoid(project()).astype(BF16)

    @pl.when(tiles("dq"))
    def _():
        qk_norm_rope(qw_ref, diff_q_scale)

    @pl.when(tiles("dk"))
    def _():
        qk_norm_rope(kw_ref, 1.0)


def _input_projection(hp, n1, w_in, rcos, rsin, dc, ds1, ds2, qw, kw, gmat, seg_of_tile,
                      lp, ret_dk, diff_dh):
    rows, d_model = hp.shape
    n_col = w_in.shape[1] // d_model
    tm = _pick_tile(lp, (1040, 640, 512, 256, 128))
    tpb = lp // tm
    kern = functools.partial(_proj_kernel, seg_of_tile, d_model, ret_dk, diff_dh,
                             ret_dk ** -0.5, diff_dh ** -0.5 * LOG2E)
    tab = pl.BlockSpec((tm, LANE), lambda i, j: (i % tpb, 0))
    vec = pl.BlockSpec((1, d_model), lambda i, j: (0, 0))
    return pl.pallas_call(
        kern,
        grid=(rows // tm, n_col),
        in_specs=[
            pl.BlockSpec((tm, d_model), lambda i, j: (i, 0)),
            vec,
            pl.BlockSpec((d_model, d_model), lambda i, j: (0, j)),
            tab, tab, tab, tab, tab,
            vec, vec,
            pl.BlockSpec((MXU, MXU), lambda i, j: (0, 0)),
        ],
        out_specs=pl.BlockSpec((tm, d_model), lambda i, j: (i, j)),
        out_shape=jax.ShapeDtypeStruct((rows, w_in.shape[1]), BF16),
        scratch_shapes=[pltpu.VMEM((tm, d_model), BF16)],
        compiler_params=pltpu.CompilerParams(
            dimension_semantics=("arbitrary", "arbitrary"), vmem_limit_bytes=VMEM_LIMIT),
        name="in_proj",
    )(hp, n1, w_in, rcos, rsin, dc, ds1, ds2, qw, kw, gmat)


def _retention_kernel(n_heads, dk, dv, q_ref, k_ref, v_ref, g_ref, intra_ref, qdec_ref,
                      kdec_ref, cdec_ref, o_ref, r_ref):
    c = pl.program_id(1)

    @pl.when(c == 0)
    def _():
        r_ref[...] = jnp.zeros_like(r_ref)

    for h in range(n_heads):
        q = q_ref[:, h * dk:(h + 1) * dk]
        k = k_ref[:, h * dk:(h + 1) * dk]
        v = v_ref[:, h * dv:(h + 1) * dv]
        r_old = r_ref[h]
        s = _dot_nt(q, k) * intra_ref[h]
        o = _dot(s.astype(BF16), v) + _dot(q, r_old.astype(BF16)) * qdec_ref[h]
        kd_t = (k.astype(F32) * kdec_ref[h]).T.astype(BF16)
        r_ref[h] = r_old * cdec_ref[h] + _dot(kd_t, v)
        mu = jnp.mean(o, axis=-1, keepdims=True)
        d = o - mu
        var = jnp.mean(d * d, axis=-1, keepdims=True)
        on = d * lax.rsqrt(var + EPS)
        o_ref[:, h * dv:(h + 1) * dv] = (g_ref[:, h * dv:(h + 1) * dv].astype(F32) * on).astype(BF16)


def _retention(proj, batch, lp, d_model, col_q, col_k, col_v, col_g):
    n_heads = RET_HEADS
    dk = d_model // n_heads
    dv = 2 * dk
    n_chunks = lp // CHUNK
    lg = jnp.log(1.0 - jnp.power(2.0, -5.0 - jnp.arange(n_heads, dtype=F32)))
    idx = jnp.arange(CHUNK, dtype=F32)
    dist = idx[:, None] - idx[None, :]
    intra = jnp.where(dist[None] >= 0, jnp.exp(lg[:, None, None] * jnp.maximum(dist, 0.0)[None]), 0.0)
    q_dec = jnp.exp(lg[:, None] * (idx[None] + 1.0))
    k_dec = jnp.exp(lg[:, None] * (CHUNK - 1.0 - idx[None]))
    chunk_dec = jnp.exp(lg * CHUNK)
    qdec = jnp.broadcast_to(q_dec[:, :, None], (n_heads, CHUNK, dv))
    kdec = jnp.broadcast_to(k_dec[:, :, None], (n_heads, CHUNK, dk))
    cdec = jnp.broadcast_to(chunk_dec[:, None, None], (n_heads, dk, dv))

    kern = functools.partial(_retention_kernel, n_heads, dk, dv)
    row = lambda b, c: b * n_chunks + c
    const3 = lambda b, c: (0, 0, 0)
    return pl.pallas_call(
        kern,
        grid=(batch, n_chunks),
        in_specs=[
            pl.BlockSpec((CHUNK, d_model), lambda b, c: (row(b, c), col_q // d_model)),
            pl.BlockSpec((CHUNK, d_model), lambda b, c: (row(b, c), col_k // d_model)),
            pl.BlockSpec((CHUNK, 2 * d_model), lambda b, c: (row(b, c), col_v // (2 * d_model))),
            pl.BlockSpec((CHUNK, 2 * d_model), lambda b, c: (row(b, c), col_g // (2 * d_model))),
            pl.BlockSpec((n_heads, CHUNK, CHUNK), const3),
            pl.BlockSpec((n_heads, CHUNK, dv), const3),
            pl.BlockSpec((n_heads, CHUNK, dk), const3),
            pl.BlockSpec((n_heads, dk, dv), const3),
        ],
        out_specs=pl.BlockSpec((CHUNK, 2 * d_model), lambda b, c: (row(b, c), 0)),
        out_shape=jax.ShapeDtypeStruct((batch * lp, 2 * d_model), BF16),
        scratch_shapes=[pltpu.VMEM((n_heads, dk, dv), F32)],
        compiler_params=pltpu.CompilerParams(
            dimension_semantics=("arbitrary", "arbitrary"), vmem_limit_bytes=VMEM_LIMIT),
        name="retention",
    )(proj, proj, proj, proj, intra, qdec, kdec, cdec)


def _lambda(lq1_ref, lk1_ref, lq2_ref, lk2_ref, lam_init):
    return (jnp.exp(jnp.sum(lq1_ref[...] * lk1_ref[...], axis=-1, keepdims=True))
            - jnp.exp(jnp.sum(lq2_ref[...] * lk2_ref[...], axis=-1, keepdims=True)) + lam_init)


def _stack_q(q, dh):
    lo = lax.broadcasted_iota(jnp.int32, q.shape, 1) < dh
    zero = jnp.zeros_like(q)
    return jnp.concatenate([jnp.where(lo, q, zero), jnp.where(lo, zero, q)], axis=0)


def _attn_bounded_kernel(lp, dh, lam_init, bound_ref, lq1_ref, lk1_ref, lq2_ref, lk2_ref, sw_ref,
                         q_ref, k_ref, v_ref, o_ref,
                         qs_ref, kp_ref, v1_ref, p0_ref, p1_ref, acc_ref, bias_ref):
    t_blk = ATT_T
    p_refs = (p0_ref, p1_ref)
    dv = 2 * dh
    lpp = kp_ref.shape[0]
    n_q = lpp // t_blk
    n_stages = n_q * (n_q + 1) // 2
    tail = lp - (n_q - 1) * t_blk
    bound = bound_ref[0]
    lam = _lambda(lq1_ref, lk1_ref, lq2_ref, lk2_ref, lam_init)
    sw = sw_ref[...] * (1.0 - lam_init)

    for blk in range(n_q - 1):
        qs_ref[blk] = _stack_q(q_ref[blk * t_blk:(blk + 1) * t_blk, :], dh)
    q_last = q_ref[(n_q - 1) * t_blk:lp, :]
    if tail < t_blk:
        q_last = jnp.concatenate([q_last, jnp.zeros((t_blk - tail, dv), BF16)], axis=0)
    qs_ref[n_q - 1] = _stack_q(q_last, dh)
    kp_ref[0:lp, :] = k_ref[...]
    kp_ref[lp:lpp, :] = jnp.zeros((lpp - lp, dv), BF16)
    v1_ref[0:PAD, :] = jnp.zeros((PAD, 2 * dv), BF16)
    v1_ref[PAD:lp, 0:dv] = v_ref[PAD:lp, :]
    v1_ref[PAD:lp, dv:2 * dv] = jnp.ones((lp - PAD, dv), BF16)
    v1_ref[lp:lpp, :] = jnp.zeros((lpp - lp, 2 * dv), BF16)

    rd = lax.broadcasted_iota(jnp.int32, (2 * t_blk, t_blk), 0) % t_blk
    cd = lax.broadcasted_iota(jnp.int32, (2 * t_blk, t_blk), 1)
    bias_ref[0] = jnp.full((2 * t_blk, t_blk), bound, F32)
    bias_ref[1] = jnp.where(cd <= rd, bound, -NEG)

    p_refs[1][...] = jnp.zeros_like(p_refs[1])
    acc_ref[...] = jnp.zeros_like(acc_ref)
    n_acc = acc_ref.shape[0]

    def finish(slot):
        acc = acc_ref[slot]
        r = acc[:, :dv] / jnp.maximum(acc[:, dv:], 1e-30)
        o = r[:t_blk] - lam * r[t_blk:]
        ms = jnp.mean(o * o, axis=-1, keepdims=True)
        return (o * lax.rsqrt(ms + EPS) * sw).astype(BF16)

    def rows(blk):
        return pl.ds(pl.multiple_of(blk * t_blk, t_blk), t_blk)

    def stage(parity, state, scores=True):
        (i, t), (i1, t1) = state
        if scores:
            s = _dot_nt(qs_ref[i], kp_ref[rows(t), :])
            p_refs[parity][...] = jnp.exp2(s - bias_ref[(t == i).astype(jnp.int32)]).astype(BF16)
        acc_ref[i1 % n_acc] += _dot(p_refs[1 - parity][...], v1_ref[rows(t1), :])
        wrap = t == i
        return ((jnp.where(wrap, i + 1, i), jnp.where(wrap, 0, t + 1)), (i, t))

    def stage_group(n, state):
        done = []
        for g in range(n):
            done.append(state[1])
            state = stage(g % 2, state)
        for (i1, t1) in done:
            @pl.when((t1 == i1) & (i1 < n_q - 1))
            def _():
                o_ref[rows(i1), :] = finish(i1 % n_acc)
                acc_ref[i1 % n_acc] = jnp.zeros(acc_ref.shape[1:], F32)
        return state

    state = ((jnp.int32(0), jnp.int32(0)), (jnp.int32(1), jnp.int32(0)))
    state = lax.fori_loop(0, n_stages // ATT_GROUP, lambda _, st: stage_group(ATT_GROUP, st), state)
    rest = n_stages % ATT_GROUP
    state = stage_group(rest, state)
    stage(rest % 2, state, scores=False)
    o_ref[(n_q - 1) * t_blk:lp, :] = finish((n_q - 1) % n_acc)[0:tail]


def _attn_online_kernel(lp, dh, lam_init, lq1_ref, lk1_ref, lq2_ref, lk2_ref, sw_ref,
                        q_ref, k_ref, v_ref, o_ref):
    tq = ATT_TQ
    n_blocks = (lp - CHUNK) // tq
    lam = _lambda(lq1_ref, lk1_ref, lq2_ref, lk2_ref, lam_init)
    sw = sw_ref[...] * (1.0 - lam_init)
    stack_q = functools.partial(_stack_q, dh=dh)

    def first_block(qs, kb, vb, mask):
        s = jnp.where(mask, _dot_nt(qs, kb), NEG)
        m = jnp.max(s, axis=-1, keepdims=True)
        p = jnp.exp2(s - m)
        l = jnp.sum(p, axis=-1, keepdims=True)
        acc = _dot(p.astype(BF16), vb)
        return m, l, acc

    def next_block(carry, qs, kb, vb, mask):
        m, l, acc = carry
        s = _dot_nt(qs, kb)
        if mask is not None:
            s = jnp.where(mask, s, NEG)
        m_new = jnp.maximum(m, jnp.max(s, axis=-1, keepdims=True))
        alpha = jnp.exp2(m - m_new)
        p = jnp.exp2(s - m_new)
        l = alpha * l + jnp.sum(p, axis=-1, keepdims=True)
        acc = alpha * acc + _dot(p.astype(BF16), vb)
        return m_new, l, acc

    def finish(carry, n):
        _, l, acc = carry
        o = acc[:n] / l[:n] - lam * (acc[n:] / l[n:])
        ms = jnp.mean(o * o, axis=-1, keepdims=True)
        return (o * lax.rsqrt(ms + EPS) * sw).astype(BF16)

    r0 = lax.broadcasted_iota(jnp.int32, (2 * CHUNK, CHUNK), 0) % CHUNK
    c0 = lax.broadcasted_iota(jnp.int32, (2 * CHUNK, CHUNK), 1)
    k0 = k_ref[0:CHUNK, :]
    v0 = v_ref[0:CHUNK, :]
    carry = first_block(stack_q(q_ref[0:CHUNK, :]), k0, v0, (c0 <= r0) & (c0 >= PAD))
    o_ref[0:CHUNK, :] = finish(carry, CHUNK)

    meta_mask = lax.broadcasted_iota(jnp.int32, (2 * tq, CHUNK), 1) >= PAD
    rd = lax.broadcasted_iota(jnp.int32, (2 * tq, tq), 0) % tq
    cd = lax.broadcasted_iota(jnp.int32, (2 * tq, tq), 1)
    diag_mask = cd <= rd

    def q_block(i, _):
        row = pl.multiple_of(CHUNK + i * tq, CHUNK)
        qs = stack_q(q_ref[pl.ds(row, tq), :])
        carry = first_block(qs, k0, v0, meta_mask)

        def k_block(jb, carry):
            kr = pl.multiple_of(CHUNK + jb * tq, CHUNK)
            return next_block(carry, qs, k_ref[pl.ds(kr, tq), :], v_ref[pl.ds(kr, tq), :], None)

        carry = lax.fori_loop(0, i, k_block, carry)
        carry = next_block(carry, qs, k_ref[pl.ds(row, tq), :], v_ref[pl.ds(row, tq), :], diag_mask)
        o_ref[pl.ds(row, tq), :] = finish(carry, tq)
        return 0

    lax.fori_loop(0, n_blocks, q_block, 0)


MAX_SAFE_SCORE_BOUND = 60.0


def _diff_attention(proj, score_bound, lq1, lk1, lq2, lk2, subw, batch, lp, d_model,
                    col_q, col_k, col_v, lam_init):
    dv = d_model // DIFF_HEADS
    dh = dv // 2
    proj3 = proj.reshape(batch, lp, proj.shape[1])
    lpp = -(-lp // ATT_T) * ATT_T
    small = lambda n: pl.BlockSpec((1, n), lambda b, h: (0, 0))
    head = lambda col: pl.BlockSpec((None, lp, dv), lambda b, h: (b, 0, col // dv + h))
    common = dict(
        grid=(batch, DIFF_HEADS),
        out_specs=pl.BlockSpec((None, lp, dv), lambda b, h: (b, 0, h)),
        out_shape=jax.ShapeDtypeStruct((batch, lp, d_model), BF16),
        compiler_params=pltpu.CompilerParams(
            dimension_semantics=("arbitrary", "arbitrary"), vmem_limit_bytes=VMEM_LIMIT),
    )
    tensors = [small(dh), small(dh), small(dh), small(dh), small(dv),
               head(col_q), head(col_k), head(col_v)]

    def bounded():
        return pl.pallas_call(
            functools.partial(_attn_bounded_kernel, lp, dh, lam_init),
            in_specs=[pl.BlockSpec(memory_space=pltpu.SMEM)] + tensors,
            scratch_shapes=[pltpu.VMEM((lpp // ATT_T, 2 * ATT_T, dv), BF16),
                            pltpu.VMEM((lpp, dv), BF16),
                            pltpu.VMEM((lpp, 2 * dv), BF16),
                            pltpu.VMEM((2 * ATT_T, ATT_T), BF16),
                            pltpu.VMEM((2 * ATT_T, ATT_T), BF16),
                            pltpu.VMEM((ATT_GROUP, 2 * ATT_T, 2 * dv), F32),
                            pltpu.VMEM((2, 2 * ATT_T, ATT_T), F32)],
            name="diff_attn", **common,
        )(score_bound, lq1, lk1, lq2, lk2, subw, proj3, proj3, proj3)

    def online():
        return pl.pallas_call(
            functools.partial(_attn_online_kernel, lp, dh, lam_init),
            in_specs=tensors, name="diff_attn_online", **common,
        )(lq1, lk1, lq2, lk2, subw, proj3, proj3, proj3)

    out = lax.cond(score_bound[0] <= MAX_SAFE_SCORE_BOUND, bounded, online)
    return out.reshape(batch * lp, d_model)


def _merge_kernel(ro_ref, do_ref, gr_ref, gd_ref, h_ref, wr_ref, wd_ref, wo_ref, n2_ref,
                  h1_ref, u2_ref):
    ro = _dot(ro_ref[...], wr_ref[...])
    do = _dot(do_ref[...], wd_ref[...])
    z = gr_ref[...].astype(F32) * ro + gd_ref[...].astype(F32) * do
    h1 = h_ref[...] + _dot(z.astype(BF16), wo_ref[...])
    h1_ref[...] = h1
    ms = jnp.mean(h1 * h1, axis=-1, keepdims=True)
    u2_ref[...] = (h1 * lax.rsqrt(ms + EPS) * n2_ref[...]).astype(BF16)


def _merge(ro, do, proj, hp, w_ret_o, w_diff_o, w_out, n2, col_gates):
    rows, d_model = hp.shape
    tm = _pick_tile(rows, (640, 512, 256, 128))
    gcol = col_gates // d_model
    const = lambda i: (0, 0)
    return pl.pallas_call(
        _merge_kernel,
        grid=(rows // tm,),
        in_specs=[
            pl.BlockSpec((tm, ro.shape[1]), lambda i: (i, 0)),
            pl.BlockSpec((tm, d_model), lambda i: (i, 0)),
            pl.BlockSpec((tm, d_model), lambda i: (i, gcol)),
            pl.BlockSpec((tm, d_model), lambda i: (i, gcol + 1)),
            pl.BlockSpec((tm, d_model), lambda i: (i, 0)),
            pl.BlockSpec(w_ret_o.shape, const),
            pl.BlockSpec(w_diff_o.shape, const),
            pl.BlockSpec(w_out.shape, const),
            pl.BlockSpec((1, d_model), const),
        ],
        out_specs=[pl.BlockSpec((tm, d_model), lambda i: (i, 0)),
                   pl.BlockSpec((tm, d_model), lambda i: (i, 0))],
        out_shape=[jax.ShapeDtypeStruct((rows, d_model), F32),
                   jax.ShapeDtypeStruct((rows, d_model), BF16)],
        compiler_params=pltpu.CompilerParams(
            dimension_semantics=("arbitrary",), vmem_limit_bytes=VMEM_LIMIT),
        name="merge",
    )(ro, do, proj, proj, hp, w_ret_o, w_diff_o, w_out, n2)


def _ffn_kernel(ffn, tc, u_ref, h_ref, wu_ref, cw_ref, cb_ref, wd_ref, o_ref,
                carry_ref, up_ref, act_ref):
    tm = u_ref.shape[0]
    halo = carry_ref.shape[0]

    @pl.when(pl.program_id(0) == 0)
    def _():
        carry_ref[...] = jnp.zeros_like(carry_ref)

    u = u_ref[...]

    def conv_cols(col):
        up_ref[0:halo, :] = carry_ref[:, col:col + tc]
        up_ref[halo:halo + tm, :] = _dot(u, wu_ref[:, col:col + tc])
        carry_ref[:, col:col + tc] = up_ref[tm:tm + halo, :]
        y = cb_ref[:, col:col + tc]
        for t in range(CONV_W):
            lo = halo - (CONV_W - 1) + t
            y = y + up_ref[lo:lo + tm, :] * cw_ref[t:t + 1, col:col + tc]
        return y

    for c in range(ffn // tc):
        a = conv_cols(c * tc)
        b = conv_cols(ffn + c * tc)
        act_ref[:, c * tc:(c + 1) * tc] = (a * _sigmoid(a) * b).astype(BF16)

    o_ref[...] = h_ref[...] + _dot(act_ref[...], wd_ref[...])


def _ffn(u2, h1, w_up, conv_w, conv_b, w_down):
    rows, d_model = h1.shape
    ffn = w_down.shape[0]
    tc = MXU
    halo = 8
    tm = _pick_tile(rows, (640, 512, 256, 128))
    const = lambda i: (0, 0)
    resident = lambda shape: pl.BlockSpec(shape, const, pipeline_mode=pl.Buffered(1))
    kern = functools.partial(_ffn_kernel, ffn, tc)
    return pl.pallas_call(
        kern,
        grid=(rows // tm,),
        in_specs=[
            pl.BlockSpec((tm, d_model), lambda i: (i, 0)),
            pl.BlockSpec((tm, d_model), lambda i: (i, 0)),
            resident(w_up.shape),
            pl.BlockSpec(conv_w.shape, const),
            pl.BlockSpec(conv_b.shape, const),
            resident(w_down.shape),
        ],
        out_specs=pl.BlockSpec((tm, d_model), lambda i: (i, 0)),
        out_shape=jax.ShapeDtypeStruct((rows, d_model), F32),
        scratch_shapes=[pltpu.VMEM((halo, 2 * ffn), F32),
                        pltpu.VMEM((tm + halo, tc), F32),
                        pltpu.VMEM((tm, ffn), BF16)],
        compiler_params=pltpu.CompilerParams(
            dimension_semantics=("arbitrary",), vmem_limit_bytes=VMEM_LIMIT),
        name="conv_ffn",
    )(u2, h1, w_up, conv_w, conv_b, w_down)


def _rope_tables(lp, ret_dk, diff_dh):
    pos = jnp.arange(lp, dtype=F32) - PAD
    half = ret_dk // 2
    inv = jnp.power(RET_THETA, -jnp.arange(half, dtype=F32) / half)
    ang = pos[:, None] * inv[None, :]
    rcos, rsin = jnp.cos(ang), jnp.sin(ang)
    rot = diff_dh // 8
    inv = jnp.power(ROPE_THETA, -jnp.arange(rot, dtype=F32) / rot)
    ang = pos[:, None] * inv[None, :]
    cos, sin = jnp.cos(ang), jnp.sin(ang)
    ones = jnp.ones((lp, diff_dh - 2 * rot), F32)
    zeros = jnp.zeros((lp, diff_dh - 2 * rot), F32)
    zr = jnp.zeros((lp, rot), F32)
    rep = LANE // diff_dh
    dc = jnp.tile(jnp.concatenate([cos, cos, ones], axis=1), (1, rep))
    ds1 = jnp.tile(jnp.concatenate([-sin, zr, zeros], axis=1), (1, rep))
    ds2 = jnp.tile(jnp.concatenate([zr, sin, zeros], axis=1), (1, rep))
    return rcos, rsin, dc, ds1, ds2


def kernel(x, meta_tokens, norm1_w, w_in, w_ret_o, q_norm_w, k_norm_w, lambda_q1, lambda_k1,
           lambda_q2, lambda_k2, diff_subln_w, w_diff_o, w_out, norm2_w, w_up, conv_w, conv_b,
           w_down):
    batch, seq, d_model = x.shape
    depth = norm1_w.shape[0]
    lp = CHUNK + seq
    ret_dk = d_model // RET_HEADS
    diff_dv = d_model // DIFF_HEADS
    diff_dh = diff_dv // 2
    assert (lp - CHUNK) % ATT_TQ == 0 and d_model % MXU == 0 and ret_dk // 2 == LANE

    seg_of_tile = ("rq", "rk", "plain", "plain", "silu", "silu", "dq", "dk", "plain",
                   "sigmoid", "sigmoid")
    col_rq, col_rk, col_rv, col_rg = 0, d_model, 2 * d_model, 4 * d_model
    col_dq, col_dk, col_dv, col_gates = 6 * d_model, 7 * d_model, 8 * d_model, 9 * d_model
    assert w_in.shape[2] == len(seg_of_tile) * d_model

    rcos, rsin, dc, ds1, ds2 = _rope_tables(lp, ret_dk, diff_dh)
    gid = np.arange(MXU) // diff_dh
    gmat = jnp.asarray(gid[:, None] == gid[None, :], dtype=BF16)

    dt = x.dtype
    hp = jnp.concatenate([jnp.zeros((batch, PAD, d_model), dt),
                          jnp.broadcast_to(meta_tokens.astype(dt)[None], (batch, N_META, d_model)),
                          x], axis=1).reshape(batch * lp, d_model)

    for l in range(depth):
        lam_init = 0.8 - 0.6 * math.exp(-0.3 * l)
        row = lambda a: a[l].astype(F32).reshape(1, -1)
        rep = d_model // diff_dh
        proj = _input_projection(
            hp, row(norm1_w), w_in[l].astype(BF16), rcos, rsin, dc, ds1, ds2,
            jnp.tile(row(q_norm_w), (1, rep)), jnp.tile(row(k_norm_w), (1, rep)), gmat,
            seg_of_tile, lp, ret_dk, diff_dh)
        ro = _retention(proj, batch, lp, d_model, col_rq, col_rk, col_rv, col_rg)
        score_bound = (jnp.max(jnp.abs(q_norm_w[l].astype(F32))) * jnp.max(jnp.abs(k_norm_w[l].astype(F32)))
                       * (diff_dh ** 0.5 * LOG2E * (1.0 + 2.0 ** -5))).reshape(1)
        do = _diff_attention(proj, score_bound, row(lambda_q1), row(lambda_k1), row(lambda_q2),
                             row(lambda_k2), row(diff_subln_w), batch, lp, d_model,
                             col_dq, col_dk, col_dv, lam_init)
        h1, u2 = _merge(ro, do, proj, hp, w_ret_o[l].astype(BF16), w_diff_o[l].astype(BF16),
                        w_out[l].astype(BF16), row(norm2_w), col_gates)
        hp = _ffn(u2, h1, w_up[l].astype(BF16), conv_w[l].astype(F32),
                  conv_b[l].astype(F32).reshape(1, -1), w_down[l].astype(BF16))

    return hp.reshape(batch, lp, d_model)[:, CHUNK:]
```

```python
import functools
import math

import numpy as np
import jax
import jax.numpy as jnp
from jax import lax
from jax.experimental import pallas as pl
from jax.experimental.pallas import tpu as pltpu

F32 = jnp.float32
BF16 = jnp.bfloat16

N_META = 16
CHUNK = 128
PAD = CHUNK - N_META
RET_HEADS = 4
RET_THETA = 10000.0
DIFF_HEADS = 8
ROPE_THETA = 500000.0
CONV_W = 3
EPS = 1e-6
NEG = -1e30
LANE = 128
MXU = 256
VMEM_LIMIT = 56 * 1024 * 1024

ATT_TQ = 256
ATT_T = 512
ATT_GROUP = 8
LOG2E = math.log2(math.e)


def _pick_tile(n, candidates):
    for c in candidates:
        if n % c == 0:
            return c
    raise ValueError(f"no tile for {n}")


def _dot(a, b):
    return jnp.dot(a, b, preferred_element_type=F32)


def _dot_nt(a, b):
    return lax.dot_general(a, b, (((1,), (1,)), ((), ())), preferred_element_type=F32)


def _sigmoid(v):
    return 0.5 * jnp.tanh(0.5 * v) + 0.5


def _proj_kernel(seg_of_tile, d_model, ret_dk, diff_dh, ret_k_scale,
                 x_ref, n1_ref, w_ref, rcos_ref, rsin_ref, qtab_ref, ktab_ref, g_ref, o_ref, u_ref):
    j = pl.program_id(1)

    @pl.when(j == 0)
    def _():
        x = x_ref[...]
        ms = jnp.mean(x * x, axis=-1, keepdims=True)
        u_ref[...] = (x * lax.rsqrt(ms + EPS) * n1_ref[...]).astype(BF16)

    half = ret_dk // 2

    def project():
        return _dot(u_ref[...], w_ref[...])

    def ret_rope(scale):
        y = project()
        cos = rcos_ref[...]
        sin = rsin_ref[...]
        for h in range(d_model // ret_dk):
            x1 = y[:, h * ret_dk:h * ret_dk + half]
            x2 = y[:, h * ret_dk + half:(h + 1) * ret_dk]
            o_ref[:, h * ret_dk:h * ret_dk + half] = ((x1 * cos - x2 * sin) * scale).astype(BF16)
            o_ref[:, h * ret_dk + half:(h + 1) * ret_dk] = ((x2 * cos + x1 * sin) * scale).astype(BF16)

    def qk_norm_rope(tab_ref):
        y = project()
        sq = (y * y).astype(BF16)
        c = tab_ref[0]
        s1 = tab_ref[1]
        s2 = tab_ref[2]
        rot = diff_dh // 8
        for s in range(d_model // MXU):
            ss = _dot(sq[:, s * MXU:(s + 1) * MXU], g_ref[...])
            inv_rms = lax.rsqrt(ss * (1.0 / diff_dh) + EPS)
            for t in range(MXU // LANE):
                lanes = slice(s * MXU + t * LANE, s * MXU + (t + 1) * LANE)
                x = y[:, lanes]
                z = x * c + pltpu.roll(x, LANE - rot, 1) * s1 + pltpu.roll(x, rot, 1) * s2
                o_ref[:, lanes] = (z * inv_rms[:, t * LANE:(t + 1) * LANE]).astype(BF16)

    def tiles(name):
        ts = [t for t, s in enumerate(seg_of_tile) if s == name]
        cond = j == ts[0]
        for t in ts[1:]:
            cond = cond | (j == t)
        return cond

    @pl.when(tiles("rq"))
    def _():
        ret_rope(1.0)

    @pl.when(tiles("rk"))
    def _():
        ret_rope(ret_k_scale)

    @pl.when(tiles("plain"))
    def _():
        o_ref[...] = project().astype(BF16)

    @pl.when(tiles("silu"))
    def _():
        y = project()
        o_ref[...] = (y * _sigmoid(y)).astype(BF16)

    @pl.when(tiles("sigmoid"))
    def _():
        o_ref[...] = _sigmoid(project()).astype(BF16)

    @pl.when(tiles("dq"))
    def _():
        qk_norm_rope(qtab_ref)

    @pl.when(tiles("dk"))
    def _():
        qk_norm_rope(ktab_ref)


def _input_projection(hp, n1, w_in, rcos, rsin, qtab, ktab, gmat, seg_of_tile, lp, ret_dk, diff_dh):
    rows, d_model = hp.shape
    n_col = w_in.shape[1] // d_model
    tm = _pick_tile(lp, (1040, 640, 512, 256, 128))
    tpb = lp // tm
    kern = functools.partial(_proj_kernel, seg_of_tile, d_model, ret_dk, diff_dh, ret_dk ** -0.5)
    tab = pl.BlockSpec((tm, LANE), lambda i, j: (i % tpb, 0))
    tab3 = pl.BlockSpec((3, tm, LANE), lambda i, j: (0, i % tpb, 0))
    vec = pl.BlockSpec((1, d_model), lambda i, j: (0, 0))
    return pl.pallas_call(
        kern,
        grid=(rows // tm, n_col),
        in_specs=[
            pl.BlockSpec((tm, d_model), lambda i, j: (i, 0)),
            vec,
            pl.BlockSpec((d_model, d_model), lambda i, j: (0, j)),
            tab, tab, tab3, tab3,
            pl.BlockSpec((MXU, MXU), lambda i, j: (0, 0)),
        ],
        out_specs=pl.BlockSpec((tm, d_model), lambda i, j: (i, j)),
        out_shape=jax.ShapeDtypeStruct((rows, w_in.shape[1]), BF16),
        scratch_shapes=[pltpu.VMEM((tm, d_model), BF16)],
        compiler_params=pltpu.CompilerParams(
            dimension_semantics=("arbitrary", "arbitrary"), vmem_limit_bytes=VMEM_LIMIT),
        name="in_proj",
    )(hp, n1, w_in, rcos, rsin, qtab, ktab, gmat)


def _retention_kernel(n_heads, dk, dv, q_ref, k_ref, v_ref, g_ref, intra_ref, qdec_ref,
                      kdec_ref, cdec_ref, o_ref, r_ref):
    c = pl.program_id(1)

    @pl.when(c == 0)
    def _():
        r_ref[...] = jnp.zeros_like(r_ref)

    for h in range(n_heads):
        q = q_ref[:, h * dk:(h + 1) * dk]
        k = k_ref[:, h * dk:(h + 1) * dk]
        v = v_ref[:, h * dv:(h + 1) * dv]
        r_old = r_ref[h]
        s = _dot_nt(q, k) * intra_ref[h]
        o = _dot(s.astype(BF16), v) + _dot(q, r_old.astype(BF16)) * qdec_ref[h]
        kd_t = (k.astype(F32) * kdec_ref[h]).T.astype(BF16)
        r_ref[h] = r_old * cdec_ref[h] + _dot(kd_t, v)
        mu = jnp.mean(o, axis=-1, keepdims=True)
        d = o - mu
        var = jnp.mean(d * d, axis=-1, keepdims=True)
        on = d * lax.rsqrt(var + EPS)
        o_ref[:, h * dv:(h + 1) * dv] = (g_ref[:, h * dv:(h + 1) * dv].astype(F32) * on).astype(BF16)


def _retention(proj, batch, lp, d_model, col_q, col_k, col_v, col_g):
    n_heads = RET_HEADS
    dk = d_model // n_heads
    dv = 2 * dk
    n_chunks = lp // CHUNK
    lg = jnp.log(1.0 - jnp.power(2.0, -5.0 - jnp.arange(n_heads, dtype=F32)))
    idx = jnp.arange(CHUNK, dtype=F32)
    dist = idx[:, None] - idx[None, :]
    intra = jnp.where(dist[None] >= 0, jnp.exp(lg[:, None, None] * jnp.maximum(dist, 0.0)[None]), 0.0)
    q_dec = jnp.exp(lg[:, None] * (idx[None] + 1.0))
    k_dec = jnp.exp(lg[:, None] * (CHUNK - 1.0 - idx[None]))
    chunk_dec = jnp.exp(lg * CHUNK)
    qdec = jnp.broadcast_to(q_dec[:, :, None], (n_heads, CHUNK, dv))
    kdec = jnp.broadcast_to(k_dec[:, :, None], (n_heads, CHUNK, dk))
    cdec = jnp.broadcast_to(chunk_dec[:, None, None], (n_heads, dk, dv))

    kern = functools.partial(_retention_kernel, n_heads, dk, dv)
    row = lambda b, c: b * n_chunks + c
    const3 = lambda b, c: (0, 0, 0)
    return pl.pallas_call(
        kern,
        grid=(batch, n_chunks),
        in_specs=[
            pl.BlockSpec((CHUNK, d_model), lambda b, c: (row(b, c), col_q // d_model)),
            pl.BlockSpec((CHUNK, d_model), lambda b, c: (row(b, c), col_k // d_model)),
            pl.BlockSpec((CHUNK, 2 * d_model), lambda b, c: (row(b, c), col_v // (2 * d_model))),
            pl.BlockSpec((CHUNK, 2 * d_model), lambda b, c: (row(b, c), col_g // (2 * d_model))),
            pl.BlockSpec((n_heads, CHUNK, CHUNK), const3),
            pl.BlockSpec((n_heads, CHUNK, dv), const3),
            pl.BlockSpec((n_heads, CHUNK, dk), const3),
            pl.BlockSpec((n_heads, dk, dv), const3),
        ],
        out_specs=pl.BlockSpec((CHUNK, 2 * d_model), lambda b, c: (row(b, c), 0)),
        out_shape=jax.ShapeDtypeStruct((batch * lp, 2 * d_model), BF16),
        scratch_shapes=[pltpu.VMEM((n_heads, dk, dv), F32)],
        compiler_params=pltpu.CompilerParams(
            dimension_semantics=("arbitrary", "arbitrary"), vmem_limit_bytes=VMEM_LIMIT),
        name="retention",
    )(proj, proj, proj, proj, intra, qdec, kdec, cdec)


def _lambda(lq1_ref, lk1_ref, lq2_ref, lk2_ref, lam_init):
    return (jnp.exp(jnp.sum(lq1_ref[...] * lk1_ref[...], axis=-1, keepdims=True))
            - jnp.exp(jnp.sum(lq2_ref[...] * lk2_ref[...], axis=-1, keepdims=True)) + lam_init)


def _stack_q(q, dh):
    lo = lax.broadcasted_iota(jnp.int32, q.shape, 1) < dh
    zero = jnp.zeros_like(q)
    return jnp.concatenate([jnp.where(lo, q, zero), jnp.where(lo, zero, q)], axis=0)


def _attn_bounded_kernel(lp, dh, lam_init, bound_ref, lq1_ref, lk1_ref, lq2_ref, lk2_ref, sw_ref,
                         q_ref, k_ref, v_ref, o_ref,
                         qs_ref, v1_ref, p0_ref, p1_ref, acc_ref, bias_ref, pt_ref):
    t_blk = ATT_T
    p_refs = (p0_ref, p1_ref)
    dv = 2 * dh
    n_q = lp // t_blk
    tail = lp - n_q * t_blk
    n_stages = n_q * (n_q + 1) // 2
    bound = bound_ref[0]
    lam = _lambda(lq1_ref, lk1_ref, lq2_ref, lk2_ref, lam_init)
    sw = sw_ref[...] * (1.0 - lam_init)

    for blk in range(n_q):
        qs_ref[blk] = _stack_q(q_ref[blk * t_blk:(blk + 1) * t_blk, :], dh)
    v1_ref[0:PAD, :] = jnp.zeros((PAD, 2 * dv), BF16)
    v1_ref[PAD:lp, 0:dv] = v_ref[PAD:lp, :]
    v1_ref[PAD:lp, dv:2 * dv] = jnp.ones((lp - PAD, dv), BF16)

    rd = lax.broadcasted_iota(jnp.int32, (2 * t_blk, t_blk), 0) % t_blk
    cd = lax.broadcasted_iota(jnp.int32, (2 * t_blk, t_blk), 1)
    bias_ref[0] = jnp.full((2 * t_blk, t_blk), bound, F32)
    bias_ref[1] = jnp.where(cd <= rd, bound, -NEG)

    p_refs[1][...] = jnp.zeros_like(p_refs[1])
    acc_ref[...] = jnp.zeros_like(acc_ref)
    n_acc = acc_ref.shape[0]

    def normalize(acc, n):
        r = acc[:, :dv] / jnp.maximum(acc[:, dv:], 1e-30)
        o = r[:n] - lam * r[n:]
        ms = jnp.mean(o * o, axis=-1, keepdims=True)
        return (o * lax.rsqrt(ms + EPS) * sw).astype(BF16)

    def finish(slot):
        return normalize(acc_ref[slot], t_blk)

    def rows(blk):
        return pl.ds(pl.multiple_of(blk * t_blk, t_blk), t_blk)

    def stage(parity, state, scores=True):
        (i, t), (i1, t1) = state
        if scores:
            s = _dot_nt(qs_ref[i], k_ref[rows(t), :])
            p_refs[parity][...] = jnp.exp2(s - bias_ref[(t == i).astype(jnp.int32)]).astype(BF16)
        acc_ref[i1 % n_acc] += _dot(p_refs[1 - parity][...], v1_ref[rows(t1), :])
        wrap = t == i
        return ((jnp.where(wrap, i + 1, i), jnp.where(wrap, 0, t + 1)), (i, t))

    def stage_group(n, state):
        done = []
        for g in range(n):
            done.append(state[1])
            state = stage(g % 2, state)
        for (i1, t1) in done:
            @pl.when((t1 == i1) & (i1 < n_q - 1))
            def _():
                o_ref[rows(i1), :] = finish(i1 % n_acc)
                acc_ref[i1 % n_acc] = jnp.zeros(acc_ref.shape[1:], F32)
        return state

    state = ((jnp.int32(0), jnp.int32(0)), (jnp.int32(1), jnp.int32(0)))
    state = lax.fori_loop(0, n_stages // ATT_GROUP, lambda _, st: stage_group(ATT_GROUP, st), state)
    rest = n_stages % ATT_GROUP
    state = stage_group(rest, state)
    stage(rest % 2, state, scores=False)
    o_ref[(n_q - 1) * t_blk:n_q * t_blk, :] = finish((n_q - 1) % n_acc)

    if tail:
        qt = _stack_q(q_ref[n_q * t_blk:lp, :], dh)
        for blk in range(n_q):
            cols = slice(blk * t_blk, (blk + 1) * t_blk)
            pt_ref[:, cols] = jnp.exp2(_dot_nt(qt, k_ref[cols, :]) - bound).astype(BF16)
        rt = lax.broadcasted_iota(jnp.int32, (2 * tail, tail), 0) % tail
        ct = lax.broadcasted_iota(jnp.int32, (2 * tail, tail), 1)
        s = _dot_nt(qt, k_ref[n_q * t_blk:lp, :])
        pt_ref[:, n_q * t_blk:lp] = jnp.exp2(s - jnp.where(ct <= rt, bound, -NEG)).astype(BF16)
        o_ref[n_q * t_blk:lp, :] = normalize(_dot(pt_ref[...], v1_ref[...]), tail)


def _attn_online_kernel(lp, dh, lam_init, lq1_ref, lk1_ref, lq2_ref, lk2_ref, sw_ref,
                        q_ref, k_ref, v_ref, o_ref):
    tq = ATT_TQ
    n_blocks = (lp - CHUNK) // tq
    lam = _lambda(lq1_ref, lk1_ref, lq2_ref, lk2_ref, lam_init)
    sw = sw_ref[...] * (1.0 - lam_init)
    stack_q = functools.partial(_stack_q, dh=dh)

    def first_block(qs, kb, vb, mask):
        s = jnp.where(mask, _dot_nt(qs, kb), NEG)
        m = jnp.max(s, axis=-1, keepdims=True)
        p = jnp.exp2(s - m)
        l = jnp.sum(p, axis=-1, keepdims=True)
        acc = _dot(p.astype(BF16), vb)
        return m, l, acc

    def next_block(carry, qs, kb, vb, mask):
        m, l, acc = carry
        s = _dot_nt(qs, kb)
        if mask is not None:
            s = jnp.where(mask, s, NEG)
        m_new = jnp.maximum(m, jnp.max(s, axis=-1, keepdims=True))
        alpha = jnp.exp2(m - m_new)
        p = jnp.exp2(s - m_new)
        l = alpha * l + jnp.sum(p, axis=-1, keepdims=True)
        acc = alpha * acc + _dot(p.astype(BF16), vb)
        return m_new, l, acc

    def finish(carry, n):
        _, l, acc = carry
        o = acc[:n] / l[:n] - lam * (acc[n:] / l[n:])
        ms = jnp.mean(o * o, axis=-1, keepdims=True)
        return (o * lax.rsqrt(ms + EPS) * sw).astype(BF16)

    r0 = lax.broadcasted_iota(jnp.int32, (2 * CHUNK, CHUNK), 0) % CHUNK
    c0 = lax.broadcasted_iota(jnp.int32, (2 * CHUNK, CHUNK), 1)
    k0 = k_ref[0:CHUNK, :]
    v0 = v_ref[0:CHUNK, :]
    carry = first_block(stack_q(q_ref[0:CHUNK, :]), k0, v0, (c0 <= r0) & (c0 >= PAD))
    o_ref[0:CHUNK, :] = finish(carry, CHUNK)

    meta_mask = lax.broadcasted_iota(jnp.int32, (2 * tq, CHUNK), 1) >= PAD
    rd = lax.broadcasted_iota(jnp.int32, (2 * tq, tq), 0) % tq
    cd = lax.broadcasted_iota(jnp.int32, (2 * tq, tq), 1)
    diag_mask = cd <= rd

    def q_block(i, _):
        row = pl.multiple_of(CHUNK + i * tq, CHUNK)
        qs = stack_q(q_ref[pl.ds(row, tq), :])
        carry = first_block(qs, k0, v0, meta_mask)

        def k_block(jb, carry):
            kr = pl.multiple_of(CHUNK + jb * tq, CHUNK)
            return next_block(carry, qs, k_ref[pl.ds(kr, tq), :], v_ref[pl.ds(kr, tq), :], None)

        carry = lax.fori_loop(0, i, k_block, carry)
        carry = next_block(carry, qs, k_ref[pl.ds(row, tq), :], v_ref[pl.ds(row, tq), :], diag_mask)
        o_ref[pl.ds(row, tq), :] = finish(carry, tq)
        return 0

    lax.fori_loop(0, n_blocks, q_block, 0)


MAX_SAFE_SCORE_BOUND = 60.0


def _diff_attention(proj, score_bound, lq1, lk1, lq2, lk2, subw, batch, lp, d_model,
                    col_q, col_k, col_v, lam_init):
    dv = d_model // DIFF_HEADS
    dh = dv // 2
    proj3 = proj.reshape(batch, lp, proj.shape[1])
    assert lp >= ATT_T and lp % ATT_T, "the kernel expects full blocks plus a short last block"
    small = lambda n: pl.BlockSpec((1, n), lambda b, h: (0, 0))
    head = lambda col: pl.BlockSpec((None, lp, dv), lambda b, h: (b, 0, col // dv + h))
    common = dict(
        grid=(batch, DIFF_HEADS),
        out_specs=pl.BlockSpec((None, lp, dv), lambda b, h: (b, 0, h)),
        out_shape=jax.ShapeDtypeStruct((batch, lp, d_model), BF16),
        compiler_params=pltpu.CompilerParams(
            dimension_semantics=("arbitrary", "arbitrary"), vmem_limit_bytes=VMEM_LIMIT),
    )
    tensors = [small(dh), small(dh), small(dh), small(dh), small(dv),
               head(col_q), head(col_k), head(col_v)]

    def bounded():
        return pl.pallas_call(
            functools.partial(_attn_bounded_kernel, lp, dh, lam_init),
            in_specs=[pl.BlockSpec(memory_space=pltpu.SMEM)] + tensors,
            scratch_shapes=[pltpu.VMEM((lp // ATT_T, 2 * ATT_T, dv), BF16),
                            pltpu.VMEM((lp, 2 * dv), BF16),
                            pltpu.VMEM((2 * ATT_T, ATT_T), BF16),
                            pltpu.VMEM((2 * ATT_T, ATT_T), BF16),
                            pltpu.VMEM((ATT_GROUP, 2 * ATT_T, 2 * dv), F32),
                            pltpu.VMEM((2, 2 * ATT_T, ATT_T), F32),
                            pltpu.VMEM((2 * (lp % ATT_T), lp), BF16)],
            name="diff_attn", **common,
        )(score_bound, lq1, lk1, lq2, lk2, subw, proj3, proj3, proj3)

    def online():
        return pl.pallas_call(
            functools.partial(_attn_online_kernel, lp, dh, lam_init),
            in_specs=tensors, name="diff_attn_online", **common,
        )(lq1, lk1, lq2, lk2, subw, proj3, proj3, proj3)

    out = lax.cond(score_bound[0] <= MAX_SAFE_SCORE_BOUND, bounded, online)
    return out.reshape(batch * lp, d_model)


def _merge_kernel(ro_ref, do_ref, gr_ref, gd_ref, h_ref, wr_ref, wd_ref, wo_ref, n2_ref,
                  h1_ref, u2_ref):
    ro = _dot(ro_ref[...], wr_ref[...])
    do = _dot(do_ref[...], wd_ref[...])
    z = gr_ref[...].astype(F32) * ro + gd_ref[...].astype(F32) * do
    h1 = h_ref[...] + _dot(z.astype(BF16), wo_ref[...])
    h1_ref[...] = h1
    ms = jnp.mean(h1 * h1, axis=-1, keepdims=True)
    u2_ref[...] = (h1 * lax.rsqrt(ms + EPS) * n2_ref[...]).astype(BF16)


def _merge(ro, do, proj, hp, w_ret_o, w_diff_o, w_out, n2, col_gates):
    rows, d_model = hp.shape
    tm = _pick_tile(rows, (640, 512, 256, 128))
    gcol = col_gates // d_model
    const = lambda i: (0, 0)
    return pl.pallas_call(
        _merge_kernel,
        grid=(rows // tm,),
        in_specs=[
            pl.BlockSpec((tm, ro.shape[1]), lambda i: (i, 0)),
            pl.BlockSpec((tm, d_model), lambda i: (i, 0)),
            pl.BlockSpec((tm, d_model), lambda i: (i, gcol)),
            pl.BlockSpec((tm, d_model), lambda i: (i, gcol + 1)),
            pl.BlockSpec((tm, d_model), lambda i: (i, 0)),
            pl.BlockSpec(w_ret_o.shape, const),
            pl.BlockSpec(w_diff_o.shape, const),
            pl.BlockSpec(w_out.shape, const),
            pl.BlockSpec((1, d_model), const),
        ],
        out_specs=[pl.BlockSpec((tm, d_model), lambda i: (i, 0)),
                   pl.BlockSpec((tm, d_model), lambda i: (i, 0))],
        out_shape=[jax.ShapeDtypeStruct((rows, d_model), F32),
                   jax.ShapeDtypeStruct((rows, d_model), BF16)],
        compiler_params=pltpu.CompilerParams(
            dimension_semantics=("arbitrary",), vmem_limit_bytes=VMEM_LIMIT),
        name="merge",
    )(ro, do, proj, proj, hp, w_ret_o, w_diff_o, w_out, n2)


def _ffn_kernel(ffn, tc, u_ref, h_ref, wu_ref, cw_ref, cb_ref, wd_ref, o_ref,
                carry_ref, up_ref, act_ref):
    tm = u_ref.shape[0]
    halo = carry_ref.shape[0]

    @pl.when(pl.program_id(0) == 0)
    def _():
        carry_ref[...] = jnp.zeros_like(carry_ref)

    u = u_ref[...]

    def conv_cols(col):
        up_ref[0:halo, :] = carry_ref[:, col:col + tc]
        up_ref[halo:halo + tm, :] = _dot(u, wu_ref[:, col:col + tc])
        carry_ref[:, col:col + tc] = up_ref[tm:tm + halo, :]
        y = cb_ref[:, col:col + tc]
        for t in range(CONV_W):
            lo = halo - (CONV_W - 1) + t
            y = y + up_ref[lo:lo + tm, :] * cw_ref[t:t + 1, col:col + tc]
        return y

    for c in range(ffn // tc):
        a = conv_cols(c * tc)
        b = conv_cols(ffn + c * tc)
        act_ref[:, c * tc:(c + 1) * tc] = (a * _sigmoid(a) * b).astype(BF16)

    o_ref[...] = h_ref[...] + _dot(act_ref[...], wd_ref[...])


def _ffn(u2, h1, w_up, conv_w, conv_b, w_down):
    rows, d_model = h1.shape
    ffn = w_down.shape[0]
    tc = MXU
    halo = 8
    tm = _pick_tile(rows, (640, 512, 256, 128))
    const = lambda i: (0, 0)
    resident = lambda shape: pl.BlockSpec(shape, const, pipeline_mode=pl.Buffered(1))
    kern = functools.partial(_ffn_kernel, ffn, tc)
    return pl.pallas_call(
        kern,
        grid=(rows // tm,),
        in_specs=[
            pl.BlockSpec((tm, d_model), lambda i: (i, 0)),
            pl.BlockSpec((tm, d_model), lambda i: (i, 0)),
            resident(w_up.shape),
            pl.BlockSpec(conv_w.shape, const),
            pl.BlockSpec(conv_b.shape, const),
            resident(w_down.shape),
        ],
        out_specs=pl.BlockSpec((tm, d_model), lambda i: (i, 0)),
        out_shape=jax.ShapeDtypeStruct((rows, d_model), F32),
        scratch_shapes=[pltpu.VMEM((halo, 2 * ffn), F32),
                        pltpu.VMEM((tm + halo, tc), F32),
                        pltpu.VMEM((tm, ffn), BF16)],
        compiler_params=pltpu.CompilerParams(
            dimension_semantics=("arbitrary",), vmem_limit_bytes=VMEM_LIMIT),
        name="conv_ffn",
    )(u2, h1, w_up, conv_w, conv_b, w_down)


def _rope_tables(lp, ret_dk, diff_dh):
    pos = jnp.arange(lp, dtype=F32) - PAD
    half = ret_dk // 2
    inv = jnp.power(RET_THETA, -jnp.arange(half, dtype=F32) / half)
    ang = pos[:, None] * inv[None, :]
    rcos, rsin = jnp.cos(ang), jnp.sin(ang)
    rot = diff_dh // 8
    inv = jnp.power(ROPE_THETA, -jnp.arange(rot, dtype=F32) / rot)
    ang = pos[:, None] * inv[None, :]
    cos, sin = jnp.cos(ang), jnp.sin(ang)
    ones = jnp.ones((lp, diff_dh - 2 * rot), F32)
    zeros = jnp.zeros((lp, diff_dh - 2 * rot), F32)
    zr = jnp.zeros((lp, rot), F32)
    rep = LANE // diff_dh
    dc = jnp.tile(jnp.concatenate([cos, cos, ones], axis=1), (1, rep))
    ds1 = jnp.tile(jnp.concatenate([-sin, zr, zeros], axis=1), (1, rep))
    ds2 = jnp.tile(jnp.concatenate([zr, sin, zeros], axis=1), (1, rep))
    return rcos, rsin, (dc, ds1, ds2), rot


def _fold_norm_weight(tables, rot, w, diff_dh, scale):
    dc, ds1, ds2 = tables
    w128 = jnp.tile(w.astype(F32).reshape(-1), LANE // diff_dh) * scale
    return jnp.stack([dc * w128, ds1 * jnp.roll(w128, -rot), ds2 * jnp.roll(w128, rot)])


def kernel(x, meta_tokens, norm1_w, w_in, w_ret_o, q_norm_w, k_norm_w, lambda_q1, lambda_k1,
           lambda_q2, lambda_k2, diff_subln_w, w_diff_o, w_out, norm2_w, w_up, conv_w, conv_b,
           w_down):
    batch, seq, d_model = x.shape
    depth = norm1_w.shape[0]
    lp = CHUNK + seq
    ret_dk = d_model // RET_HEADS
    diff_dv = d_model // DIFF_HEADS
    diff_dh = diff_dv // 2
    assert (lp - CHUNK) % ATT_TQ == 0 and d_model % MXU == 0 and ret_dk // 2 == LANE

    seg_of_tile = ("rq", "rk", "plain", "plain", "silu", "silu", "dq", "dk", "plain",
                   "sigmoid", "sigmoid")
    col_rq, col_rk, col_rv, col_rg = 0, d_model, 2 * d_model, 4 * d_model
    col_dq, col_dk, col_dv, col_gates = 6 * d_model, 7 * d_model, 8 * d_model, 9 * d_model
    assert w_in.shape[2] == len(seg_of_tile) * d_model

    rcos, rsin, diff_tables, rot = _rope_tables(lp, ret_dk, diff_dh)
    gid = np.arange(MXU) // diff_dh
    gmat = jnp.asarray(gid[:, None] == gid[None, :], dtype=BF16)

    dt = x.dtype
    hp = jnp.concatenate([jnp.zeros((batch, PAD, d_model), dt),
                          jnp.broadcast_to(meta_tokens.astype(dt)[None], (batch, N_META, d_model)),
                          x], axis=1).reshape(batch * lp, d_model)

    for l in range(depth):
        lam_init = 0.8 - 0.6 * math.exp(-0.3 * l)
        row = lambda a: a[l].astype(F32).reshape(1, -1)
        qtab = _fold_norm_weight(diff_tables, rot, q_norm_w[l], diff_dh, diff_dh ** -0.5 * LOG2E)
        ktab = _fold_norm_weight(diff_tables, rot, k_norm_w[l], diff_dh, 1.0)
        proj = _input_projection(hp, row(norm1_w), w_in[l].astype(BF16), rcos, rsin, qtab, ktab,
                                 gmat, seg_of_tile, lp, ret_dk, diff_dh)
        ro = _retention(proj, batch, lp, d_model, col_rq, col_rk, col_rv, col_rg)
        score_bound = (jnp.max(jnp.abs(q_norm_w[l].astype(F32))) * jnp.max(jnp.abs(k_norm_w[l].astype(F32)))
                       * (diff_dh ** 0.5 * LOG2E * (1.0 + 2.0 ** -5))).reshape(1)
        do = _diff_attention(proj, score_bound, row(lambda_q1), row(lambda_k1), row(lambda_q2),
                             row(lambda_k2), row(diff_subln_w), batch, lp, d_model,
                             col_dq, col_dk, col_dv, lam_init)
        h1, u2 = _merge(ro, do, proj, hp, w_ret_o[l].astype(BF16), w_diff_o[l].astype(BF16),
                        w_out[l].astype(BF16), row(norm2_w), col_gates)
        hp = _ffn(u2, h1, w_up[l].astype(BF16), conv_w[l].astype(F32),
                  conv_b[l].astype(F32).reshape(1, -1), w_down[l].astype(BF16))

    return hp.reshape(batch, lp, d_model)[:, CHUNK:]
```

```python
import functools
import math

import numpy as np
import jax
import jax.numpy as jnp
from jax import lax
from jax.experimental import pallas as pl
from jax.experimental.pallas import tpu as pltpu

F32 = jnp.float32
BF16 = jnp.bfloat16

N_META = 16
CHUNK = 128
PAD = CHUNK - N_META
RET_HEADS = 4
RET_THETA = 10000.0
DIFF_HEADS = 8
ROPE_THETA = 500000.0
CONV_W = 3
EPS = 1e-6
NEG = -1e30
LANE = 128
MXU = 256
VMEM_LIMIT = 56 * 1024 * 1024

ATT_TQ = 256
ATT_T = 512
ATT_GROUP = 8
LOG2E = math.log2(math.e)


def _pick_tile(n, candidates):
    for c in candidates:
        if n % c == 0:
            return c
    raise ValueError(f"no tile for {n}")


def _dot(a, b):
    return jnp.dot(a, b, preferred_element_type=F32)


def _dot_nt(a, b):
    return lax.dot_general(a, b, (((1,), (1,)), ((), ())), preferred_element_type=F32)


def _sigmoid(v):
    return 0.5 * jnp.tanh(0.5 * v) + 0.5


def _padded_tile_copy(x_hbm, buf_ref, sem, tile, tiles_per_batch, fn):
    tm = buf_ref.shape[0]
    b = lax.div(tile, tiles_per_batch)
    t = lax.rem(tile, tiles_per_batch)

    @pl.when(t == 0)
    def _():
        fn(pltpu.make_async_copy(x_hbm.at[b, pl.ds(0, tm - CHUNK)],
                                 buf_ref.at[pl.ds(CHUNK, tm - CHUNK)], sem))

    @pl.when(t != 0)
    def _():
        start = pl.multiple_of(t * tm - CHUNK, 16)
        fn(pltpu.make_async_copy(x_hbm.at[b, pl.ds(start, tm)], buf_ref, sem))


def _row_tile(lp, limit):
    return max(t for t in range(CHUNK + 16, limit + 1, 16) if lp % t == 0)


def _proj_kernel(seg_of_tile, d_model, ret_dk, diff_dh, ret_k_scale, tiles_per_batch,
                 x_hbm, meta_ref, n1_ref, w_ref, rcos_ref, rsin_ref, qtab_ref, ktab_ref, g_ref,
                 o_ref, u_ref, xbuf_ref, sem_ref):
    i = pl.program_id(0)
    j = pl.program_id(1)
    fetch = functools.partial(_padded_tile_copy, x_hbm, xbuf_ref, sem_ref.at[0],
                              tiles_per_batch=tiles_per_batch)

    @pl.when(j == 0)
    def _():
        @pl.when(i == 0)
        def _():
            fetch(tile=i, fn=lambda c: c.start())

        fetch(tile=i, fn=lambda c: c.wait())

        @pl.when(lax.rem(i, tiles_per_batch) == 0)
        def _():
            xbuf_ref[0:CHUNK, :] = meta_ref[...]

        x = xbuf_ref[...]
        ms = jnp.mean(x * x, axis=-1, keepdims=True)
        u_ref[...] = (x * lax.rsqrt(ms + EPS) * n1_ref[...]).astype(BF16)

        @pl.when(i + 1 < pl.num_programs(0))
        def _():
            fetch(tile=i + 1, fn=lambda c: c.start())

    half = ret_dk // 2

    def project():
        return _dot(u_ref[...], w_ref[...])

    def ret_rope(scale):
        y = project()
        cos = rcos_ref[...]
        sin = rsin_ref[...]
        for h in range(d_model // ret_dk):
            x1 = y[:, h * ret_dk:h * ret_dk + half]
            x2 = y[:, h * ret_dk + half:(h + 1) * ret_dk]
            o_ref[:, h * ret_dk:h * ret_dk + half] = ((x1 * cos - x2 * sin) * scale).astype(BF16)
            o_ref[:, h * ret_dk + half:(h + 1) * ret_dk] = ((x2 * cos + x1 * sin) * scale).astype(BF16)

    def qk_norm_rope(tab_ref):
        y = project()
        sq = (y * y).astype(BF16)
        c = tab_ref[0]
        s1 = tab_ref[1]
        s2 = tab_ref[2]
        rot = diff_dh // 8
        for s in range(d_model // MXU):
            ss = _dot(sq[:, s * MXU:(s + 1) * MXU], g_ref[...])
            inv_rms = lax.rsqrt(ss * (1.0 / diff_dh) + EPS)
            for t in range(MXU // LANE):
                lanes = slice(s * MXU + t * LANE, s * MXU + (t + 1) * LANE)
                x = y[:, lanes]
                z = x * c + pltpu.roll(x, LANE - rot, 1) * s1 + pltpu.roll(x, rot, 1) * s2
                o_ref[:, lanes] = (z * inv_rms[:, t * LANE:(t + 1) * LANE]).astype(BF16)

    def tiles(name):
        ts = [t for t, s in enumerate(seg_of_tile) if s == name]
        cond = j == ts[0]
        for t in ts[1:]:
            cond = cond | (j == t)
        return cond

    @pl.when(tiles("rq"))
    def _():
        ret_rope(1.0)

    @pl.when(tiles("rk"))
    def _():
        ret_rope(ret_k_scale)

    @pl.when(tiles("plain"))
    def _():
        o_ref[...] = project().astype(BF16)

    @pl.when(tiles("silu"))
    def _():
        y = project()
        o_ref[...] = (y * _sigmoid(y)).astype(BF16)

    @pl.when(tiles("sigmoid"))
    def _():
        o_ref[...] = _sigmoid(project()).astype(BF16)

    @pl.when(tiles("dq"))
    def _():
        qk_norm_rope(qtab_ref)

    @pl.when(tiles("dk"))
    def _():
        qk_norm_rope(ktab_ref)


def _input_projection(x, meta_chunk, n1, w_in, rcos, rsin, qtab, ktab, gmat, seg_of_tile, lp,
                      ret_dk, diff_dh):
    batch, _, d_model = x.shape
    rows = batch * lp
    n_col = w_in.shape[1] // d_model
    tm = _row_tile(lp, 1040)
    tpb = lp // tm
    kern = functools.partial(_proj_kernel, seg_of_tile, d_model, ret_dk, diff_dh, ret_dk ** -0.5, tpb)
    tab = pl.BlockSpec((tm, LANE), lambda i, j: (i % tpb, 0))
    tab3 = pl.BlockSpec((3, tm, LANE), lambda i, j: (0, i % tpb, 0))
    vec = pl.BlockSpec((1, d_model), lambda i, j: (0, 0))
    return pl.pallas_call(
        kern,
        grid=(rows // tm, n_col),
        in_specs=[
            pl.BlockSpec(memory_space=pl.ANY),
            pl.BlockSpec((CHUNK, d_model), lambda i, j: (0, 0)),
            vec,
            pl.BlockSpec((d_model, d_model), lambda i, j: (0, j)),
            tab, tab, tab3, tab3,
            pl.BlockSpec((MXU, MXU), lambda i, j: (0, 0)),
        ],
        out_specs=pl.BlockSpec((tm, d_model), lambda i, j: (i, j)),
        out_shape=jax.ShapeDtypeStruct((rows, w_in.shape[1]), BF16),
        scratch_shapes=[pltpu.VMEM((tm, d_model), BF16),
                        pltpu.VMEM((tm, d_model), x.dtype),
                        pltpu.SemaphoreType.DMA((1,))],
        compiler_params=pltpu.CompilerParams(
            dimension_semantics=("arbitrary", "arbitrary"), vmem_limit_bytes=VMEM_LIMIT),
        name="in_proj",
    )(x, meta_chunk, n1, w_in, rcos, rsin, qtab, ktab, gmat)


def _retention_kernel(n_heads, dk, dv, q_ref, k_ref, v_ref, g_ref, intra_ref, qdec_ref,
                      kdec_ref, cdec_ref, o_ref, r_ref):
    c = pl.program_id(1)

    @pl.when(c == 0)
    def _():
        r_ref[...] = jnp.zeros_like(r_ref)

    for cc in range(q_ref.shape[0] // CHUNK):
        rows = slice(cc * CHUNK, (cc + 1) * CHUNK)
        for h in range(n_heads):
            q = q_ref[rows, h * dk:(h + 1) * dk]
            k = k_ref[rows, h * dk:(h + 1) * dk]
            v = v_ref[rows, h * dv:(h + 1) * dv]
            r_old = r_ref[h]
            s = _dot_nt(q, k) * intra_ref[h]
            o = _dot(s.astype(BF16), v) + _dot(q, r_old.astype(BF16)) * qdec_ref[h]
            kd_t = (k.astype(F32) * kdec_ref[h]).T.astype(BF16)
            r_ref[h] = r_old * cdec_ref[h] + _dot(kd_t, v)
            mu = jnp.mean(o, axis=-1, keepdims=True)
            d = o - mu
            var = jnp.mean(d * d, axis=-1, keepdims=True)
            on = d * lax.rsqrt(var + EPS)
            gate = g_ref[rows, h * dv:(h + 1) * dv].astype(F32)
            o_ref[rows, h * dv:(h + 1) * dv] = (gate * on).astype(BF16)


def _retention(proj, batch, lp, d_model, col_q, col_k, col_v, col_g):
    n_heads = RET_HEADS
    dk = d_model // n_heads
    dv = 2 * dk
    n_chunks = lp // CHUNK
    lg = jnp.log(1.0 - jnp.power(2.0, -5.0 - jnp.arange(n_heads, dtype=F32)))
    idx = jnp.arange(CHUNK, dtype=F32)
    dist = idx[:, None] - idx[None, :]
    intra = jnp.where(dist[None] >= 0, jnp.exp(lg[:, None, None] * jnp.maximum(dist, 0.0)[None]), 0.0)
    q_dec = jnp.exp(lg[:, None] * (idx[None] + 1.0))
    k_dec = jnp.exp(lg[:, None] * (CHUNK - 1.0 - idx[None]))
    chunk_dec = jnp.exp(lg * CHUNK)
    qdec = jnp.broadcast_to(q_dec[:, :, None], (n_heads, CHUNK, dv))
    kdec = jnp.broadcast_to(k_dec[:, :, None], (n_heads, CHUNK, dk))
    cdec = jnp.broadcast_to(chunk_dec[:, None, None], (n_heads, dk, dv))

    kern = functools.partial(_retention_kernel, n_heads, dk, dv)
    per_step = _pick_tile(n_chunks, (5, 4, 3, 2, 1))
    n_steps = n_chunks // per_step
    tm = per_step * CHUNK
    row = lambda b, c: b * n_steps + c
    const3 = lambda b, c: (0, 0, 0)
    return pl.pallas_call(
        kern,
        grid=(batch, n_steps),
        in_specs=[
            pl.BlockSpec((tm, d_model), lambda b, c: (row(b, c), col_q // d_model)),
            pl.BlockSpec((tm, d_model), lambda b, c: (row(b, c), col_k // d_model)),
            pl.BlockSpec((tm, 2 * d_model), lambda b, c: (row(b, c), col_v // (2 * d_model))),
            pl.BlockSpec((tm, 2 * d_model), lambda b, c: (row(b, c), col_g // (2 * d_model))),
            pl.BlockSpec((n_heads, CHUNK, CHUNK), const3),
            pl.BlockSpec((n_heads, CHUNK, dv), const3),
            pl.BlockSpec((n_heads, CHUNK, dk), const3),
            pl.BlockSpec((n_heads, dk, dv), const3),
        ],
        out_specs=pl.BlockSpec((tm, 2 * d_model), lambda b, c: (row(b, c), 0)),
        out_shape=jax.ShapeDtypeStruct((batch * lp, 2 * d_model), BF16),
        scratch_shapes=[pltpu.VMEM((n_heads, dk, dv), F32)],
        compiler_params=pltpu.CompilerParams(
            dimension_semantics=("arbitrary", "arbitrary"), vmem_limit_bytes=VMEM_LIMIT),
        name="retention",
    )(proj, proj, proj, proj, intra, qdec, kdec, cdec)


def _lambda(lq1_ref, lk1_ref, lq2_ref, lk2_ref, lam_init):
    return (jnp.exp(jnp.sum(lq1_ref[...] * lk1_ref[...], axis=-1, keepdims=True))
            - jnp.exp(jnp.sum(lq2_ref[...] * lk2_ref[...], axis=-1, keepdims=True)) + lam_init)


def _stack_q(q, dh):
    lo = lax.broadcasted_iota(jnp.int32, q.shape, 1) < dh
    zero = jnp.zeros_like(q)
    return jnp.concatenate([jnp.where(lo, q, zero), jnp.where(lo, zero, q)], axis=0)


def _attn_bounded_kernel(lp, dh, lam_init, bound_ref, lq1_ref, lk1_ref, lq2_ref, lk2_ref, sw_ref,
                         q_ref, k_ref, v_ref, o_ref,
                         qs_ref, v1_ref, p0_ref, p1_ref, acc_ref, bias_ref, pt_ref):
    t_blk = ATT_T
    p_refs = (p0_ref, p1_ref)
    dv = 2 * dh
    n_q = lp // t_blk
    tail = lp - n_q * t_blk
    n_stages = n_q * (n_q + 1) // 2
    bound = bound_ref[0]
    lam = _lambda(lq1_ref, lk1_ref, lq2_ref, lk2_ref, lam_init)
    sw = sw_ref[...] * (1.0 - lam_init)

    for blk in range(n_q):
        qs_ref[blk] = _stack_q(q_ref[blk * t_blk:(blk + 1) * t_blk, :], dh)
    v1_ref[0:PAD, :] = jnp.zeros((PAD, 2 * dv), BF16)
    v1_ref[PAD:lp, 0:dv] = v_ref[PAD:lp, :]
    v1_ref[PAD:lp, dv:2 * dv] = jnp.ones((lp - PAD, dv), BF16)

    rd = lax.broadcasted_iota(jnp.int32, (2 * t_blk, t_blk), 0) % t_blk
    cd = lax.broadcasted_iota(jnp.int32, (2 * t_blk, t_blk), 1)
    bias_ref[0] = jnp.full((2 * t_blk, t_blk), bound, F32)
    bias_ref[1] = jnp.where(cd <= rd, bound, -NEG)

    p_refs[1][...] = jnp.zeros_like(p_refs[1])
    acc_ref[...] = jnp.zeros_like(acc_ref)
    n_acc = acc_ref.shape[0]

    def normalize(acc, n):
        r = acc[:, :dv] / jnp.maximum(acc[:, dv:], 1e-30)
        o = r[:n] - lam * r[n:]
        ms = jnp.mean(o * o, axis=-1, keepdims=True)
        return (o * lax.rsqrt(ms + EPS) * sw).astype(BF16)

    def finish(slot):
        return normalize(acc_ref[slot], t_blk)

    def rows(blk):
        return pl.ds(pl.multiple_of(blk * t_blk, t_blk), t_blk)

    def stage(parity, state, scores=True):
        (i, t), (i1, t1) = state
        if scores:
            s = _dot_nt(qs_ref[i], k_ref[rows(t), :])
            p_refs[parity][...] = jnp.exp2(s - bias_ref[(t == i).astype(jnp.int32)]).astype(BF16)
        acc_ref[i1 % n_acc] += _dot(p_refs[1 - parity][...], v1_ref[rows(t1), :])
        wrap = t == i
        return ((jnp.where(wrap, i + 1, i), jnp.where(wrap, 0, t + 1)), (i, t))

    def stage_group(n, state):
        done = []
        for g in range(n):
            done.append(state[1])
            state = stage(g % 2, state)
        for (i1, t1) in done:
            @pl.when((t1 == i1) & (i1 < n_q - 1))
            def _():
                o_ref[rows(i1), :] = finish(i1 % n_acc)
                acc_ref[i1 % n_acc] = jnp.zeros(acc_ref.shape[1:], F32)
        return state

    state = ((jnp.int32(0), jnp.int32(0)), (jnp.int32(1), jnp.int32(0)))
    state = lax.fori_loop(0, n_stages // ATT_GROUP, lambda _, st: stage_group(ATT_GROUP, st), state)
    rest = n_stages % ATT_GROUP
    state = stage_group(rest, state)
    stage(rest % 2, state, scores=False)
    o_ref[(n_q - 1) * t_blk:n_q * t_blk, :] = finish((n_q - 1) % n_acc)

    if tail:
        qt = _stack_q(q_ref[n_q * t_blk:lp, :], dh)
        for blk in range(n_q):
            cols = slice(blk * t_blk, (blk + 1) * t_blk)
            pt_ref[:, cols] = jnp.exp2(_dot_nt(qt, k_ref[cols, :]) - bound).astype(BF16)
        rt = lax.broadcasted_iota(jnp.int32, (2 * tail, tail), 0) % tail
        ct = lax.broadcasted_iota(jnp.int32, (2 * tail, tail), 1)
        s = _dot_nt(qt, k_ref[n_q * t_blk:lp, :])
        pt_ref[:, n_q * t_blk:lp] = jnp.exp2(s - jnp.where(ct <= rt, bound, -NEG)).astype(BF16)
        o_ref[n_q * t_blk:lp, :] = normalize(_dot(pt_ref[...], v1_ref[...]), tail)


def _attn_online_kernel(lp, dh, lam_init, lq1_ref, lk1_ref, lq2_ref, lk2_ref, sw_ref,
                        q_ref, k_ref, v_ref, o_ref):
    tq = ATT_TQ
    n_blocks = (lp - CHUNK) // tq
    lam = _lambda(lq1_ref, lk1_ref, lq2_ref, lk2_ref, lam_init)
    sw = sw_ref[...] * (1.0 - lam_init)
    stack_q = functools.partial(_stack_q, dh=dh)

    def first_block(qs, kb, vb, mask):
        s = jnp.where(mask, _dot_nt(qs, kb), NEG)
        m = jnp.max(s, axis=-1, keepdims=True)
        p = jnp.exp2(s - m)
        l = jnp.sum(p, axis=-1, keepdims=True)
        acc = _dot(p.astype(BF16), vb)
        return m, l, acc

    def next_block(carry, qs, kb, vb, mask):
        m, l, acc = carry
        s = _dot_nt(qs, kb)
        if mask is not None:
            s = jnp.where(mask, s, NEG)
        m_new = jnp.maximum(m, jnp.max(s, axis=-1, keepdims=True))
        alpha = jnp.exp2(m - m_new)
        p = jnp.exp2(s - m_new)
        l = alpha * l + jnp.sum(p, axis=-1, keepdims=True)
        acc = alpha * acc + _dot(p.astype(BF16), vb)
        return m_new, l, acc

    def finish(carry, n):
        _, l, acc = carry
        o = acc[:n] / l[:n] - lam * (acc[n:] / l[n:])
        ms = jnp.mean(o * o, axis=-1, keepdims=True)
        return (o * lax.rsqrt(ms + EPS) * sw).astype(BF16)

    r0 = lax.broadcasted_iota(jnp.int32, (2 * CHUNK, CHUNK), 0) % CHUNK
    c0 = lax.broadcasted_iota(jnp.int32, (2 * CHUNK, CHUNK), 1)
    k0 = k_ref[0:CHUNK, :]
    v0 = v_ref[0:CHUNK, :]
    carry = first_block(stack_q(q_ref[0:CHUNK, :]), k0, v0, (c0 <= r0) & (c0 >= PAD))
    o_ref[0:CHUNK, :] = finish(carry, CHUNK)

    meta_mask = lax.broadcasted_iota(jnp.int32, (2 * tq, CHUNK), 1) >= PAD
    rd = lax.broadcasted_iota(jnp.int32, (2 * tq, tq), 0) % tq
    cd = lax.broadcasted_iota(jnp.int32, (2 * tq, tq), 1)
    diag_mask = cd <= rd

    def q_block(i, _):
        row = pl.multiple_of(CHUNK + i * tq, CHUNK)
        qs = stack_q(q_ref[pl.ds(row, tq), :])
        carry = first_block(qs, k0, v0, meta_mask)

        def k_block(jb, carry):
            kr = pl.multiple_of(CHUNK + jb * tq, CHUNK)
            return next_block(carry, qs, k_ref[pl.ds(kr, tq), :], v_ref[pl.ds(kr, tq), :], None)

        carry = lax.fori_loop(0, i, k_block, carry)
        carry = next_block(carry, qs, k_ref[pl.ds(row, tq), :], v_ref[pl.ds(row, tq), :], diag_mask)
        o_ref[pl.ds(row, tq), :] = finish(carry, tq)
        return 0

    lax.fori_loop(0, n_blocks, q_block, 0)


MAX_SAFE_SCORE_BOUND = 60.0


def _diff_attention(proj, score_bound, lq1, lk1, lq2, lk2, subw, batch, lp, d_model,
                    col_q, col_k, col_v, lam_init):
    dv = d_model // DIFF_HEADS
    dh = dv // 2
    proj3 = proj.reshape(batch, lp, proj.shape[1])
    assert lp >= ATT_T and lp % ATT_T, "the kernel expects full blocks plus a short last block"
    small = lambda n: pl.BlockSpec((1, n), lambda b, h: (0, 0))
    head = lambda col: pl.BlockSpec((None, lp, dv), lambda b, h: (b, 0, col // dv + h))
    common = dict(
        grid=(batch, DIFF_HEADS),
        out_specs=pl.BlockSpec((None, lp, dv), lambda b, h: (b, 0, h)),
        out_shape=jax.ShapeDtypeStruct((batch, lp, d_model), BF16),
        compiler_params=pltpu.CompilerParams(
            dimension_semantics=("arbitrary", "arbitrary"), vmem_limit_bytes=VMEM_LIMIT),
    )
    tensors = [small(dh), small(dh), small(dh), small(dh), small(dv),
               head(col_q), head(col_k), head(col_v)]

    def bounded():
        return pl.pallas_call(
            functools.partial(_attn_bounded_kernel, lp, dh, lam_init),
            in_specs=[pl.BlockSpec(memory_space=pltpu.SMEM)] + tensors,
            scratch_shapes=[pltpu.VMEM((lp // ATT_T, 2 * ATT_T, dv), BF16),
                            pltpu.VMEM((lp, 2 * dv), BF16),
                            pltpu.VMEM((2 * ATT_T, ATT_T), BF16),
                            pltpu.VMEM((2 * ATT_T, ATT_T), BF16),
                            pltpu.VMEM((ATT_GROUP, 2 * ATT_T, 2 * dv), F32),
                            pltpu.VMEM((2, 2 * ATT_T, ATT_T), F32),
                            pltpu.VMEM((2 * (lp % ATT_T), lp), BF16)],
            name="diff_attn", **common,
        )(score_bound, lq1, lk1, lq2, lk2, subw, proj3, proj3, proj3)

    def online():
        return pl.pallas_call(
            functools.partial(_attn_online_kernel, lp, dh, lam_init),
            in_specs=tensors, name="diff_attn_online", **common,
        )(lq1, lk1, lq2, lk2, subw, proj3, proj3, proj3)

    out = lax.cond(score_bound[0] <= MAX_SAFE_SCORE_BOUND, bounded, online)
    return out.reshape(batch * lp, d_model)


def _merge_kernel(tiles_per_batch, ro_ref, do_ref, gr_ref, gd_ref, x_hbm, meta_ref, wr_ref, wd_ref,
                  wo_ref, n2_ref, h1_ref, u2_ref, hbuf_ref, sem_ref):
    step = pl.program_id(0)

    def fetch(tile, fn):
        slot = lax.rem(tile, 2)
        _padded_tile_copy(x_hbm, hbuf_ref.at[slot], sem_ref.at[slot], tile, tiles_per_batch, fn)

    @pl.when(step == 0)
    def _():
        fetch(step, lambda c: c.start())

    @pl.when(step + 1 < pl.num_programs(0))
    def _():
        fetch(step + 1, lambda c: c.start())

    ro = _dot(ro_ref[...], wr_ref[...])
    do = _dot(do_ref[...], wd_ref[...])
    z = gr_ref[...].astype(F32) * ro + gd_ref[...].astype(F32) * do
    mix = _dot(z.astype(BF16), wo_ref[...])

    fetch(step, lambda c: c.wait())
    slot = lax.rem(step, 2)

    @pl.when(lax.rem(step, tiles_per_batch) == 0)
    def _():
        hbuf_ref[slot, 0:CHUNK, :] = meta_ref[...]

    h1 = hbuf_ref[slot] + mix
    h1_ref[...] = h1
    ms = jnp.mean(h1 * h1, axis=-1, keepdims=True)
    u2_ref[...] = (h1 * lax.rsqrt(ms + EPS) * n2_ref[...]).astype(BF16)


def _merge(ro, do, proj, x, meta_chunk, w_ret_o, w_diff_o, w_out, n2, col_gates, lp):
    batch, _, d_model = x.shape
    rows = batch * lp
    tm = _row_tile(lp, 640)
    gcol = col_gates // d_model
    const = lambda i: (0, 0)
    return pl.pallas_call(
        functools.partial(_merge_kernel, lp // tm),
        grid=(rows // tm,),
        in_specs=[
            pl.BlockSpec((tm, ro.shape[1]), lambda i: (i, 0)),
            pl.BlockSpec((tm, d_model), lambda i: (i, 0)),
            pl.BlockSpec((tm, d_model), lambda i: (i, gcol)),
            pl.BlockSpec((tm, d_model), lambda i: (i, gcol + 1)),
            pl.BlockSpec(memory_space=pl.ANY),
            pl.BlockSpec((CHUNK, d_model), const),
            pl.BlockSpec(w_ret_o.shape, const),
            pl.BlockSpec(w_diff_o.shape, const),
            pl.BlockSpec(w_out.shape, const),
            pl.BlockSpec((1, d_model), const),
        ],
        out_specs=[pl.BlockSpec((tm, d_model), lambda i: (i, 0)),
                   pl.BlockSpec((tm, d_model), lambda i: (i, 0))],
        out_shape=[jax.ShapeDtypeStruct((rows, d_model), F32),
                   jax.ShapeDtypeStruct((rows, d_model), BF16)],
        scratch_shapes=[pltpu.VMEM((2, tm, d_model), x.dtype),
                        pltpu.SemaphoreType.DMA((2,))],
        compiler_params=pltpu.CompilerParams(
            dimension_semantics=("arbitrary",), vmem_limit_bytes=VMEM_LIMIT),
        name="merge",
    )(ro, do, proj, proj, x, meta_chunk, w_ret_o, w_diff_o, w_out, n2)


def _ffn_kernel(ffn, tc, tiles_per_batch, u_ref, h_ref, wu_ref, cw_ref, cb_ref, wd_ref, o_hbm,
                carry_ref, up_ref, act_ref, obuf_ref, sem_ref):
    tm = u_ref.shape[0]
    halo = carry_ref.shape[0]
    step = pl.program_id(0)
    n_steps = pl.num_programs(0)

    def out_copy(s, first):
        slot = lax.rem(s, 2)
        b = lax.div(s, tiles_per_batch)
        if first:
            return pltpu.make_async_copy(obuf_ref.at[slot, pl.ds(CHUNK, tm - CHUNK)],
                                         o_hbm.at[b, pl.ds(0, tm - CHUNK)], sem_ref.at[slot])
        start = pl.multiple_of(lax.rem(s, tiles_per_batch) * tm - CHUNK, CHUNK)
        return pltpu.make_async_copy(obuf_ref.at[slot], o_hbm.at[b, pl.ds(start, tm)], sem_ref.at[slot])

    def for_copy(s, fn):
        first = lax.rem(s, tiles_per_batch) == 0

        @pl.when(first)
        def _():
            fn(out_copy(s, True))

        @pl.when(jnp.logical_not(first))
        def _():
            fn(out_copy(s, False))

    @pl.when(step == 0)
    def _():
        carry_ref[...] = jnp.zeros_like(carry_ref)

    @pl.when(step >= 2)
    def _():
        for_copy(step - 2, lambda c: c.wait())

    u = u_ref[...]

    def conv_cols(col):
        up_ref[0:halo, :] = carry_ref[:, col:col + tc]
        up_ref[halo:halo + tm, :] = _dot(u, wu_ref[:, col:col + tc])
        carry_ref[:, col:col + tc] = up_ref[tm:tm + halo, :]
        y = cb_ref[:, col:col + tc]
        for t in range(CONV_W):
            lo = halo - (CONV_W - 1) + t
            y = y + up_ref[lo:lo + tm, :] * cw_ref[t:t + 1, col:col + tc]
        return y

    for c in range(ffn // tc):
        a = conv_cols(c * tc)
        b = conv_cols(ffn + c * tc)
        act_ref[:, c * tc:(c + 1) * tc] = (a * _sigmoid(a) * b).astype(BF16)

    obuf_ref[lax.rem(step, 2)] = h_ref[...] + _dot(act_ref[...], wd_ref[...])
    for_copy(step, lambda c: c.start())

    @pl.when(step == n_steps - 1)
    def _():
        @pl.when(step >= 1)
        def _():
            for_copy(step - 1, lambda c: c.wait())

        for_copy(step, lambda c: c.wait())


def _ffn(u2, h1, w_up, conv_w, conv_b, w_down, batch, lp):
    rows, d_model = h1.shape
    ffn = w_down.shape[0]
    tc = MXU
    halo = 8
    tm = _row_tile(lp, 640)
    const = lambda i: (0, 0)
    resident = lambda shape: pl.BlockSpec(shape, const, pipeline_mode=pl.Buffered(1))
    kern = functools.partial(_ffn_kernel, ffn, tc, lp // tm)
    return pl.pallas_call(
        kern,
        grid=(rows // tm,),
        in_specs=[
            pl.BlockSpec((tm, d_model), lambda i: (i, 0)),
            pl.BlockSpec((tm, d_model), lambda i: (i, 0)),
            resident(w_up.shape),
            pl.BlockSpec(conv_w.shape, const),
            pl.BlockSpec(conv_b.shape, const),
            resident(w_down.shape),
        ],
        out_specs=pl.BlockSpec(memory_space=pl.ANY),
        out_shape=jax.ShapeDtypeStruct((batch, lp - CHUNK, d_model), F32),
        scratch_shapes=[pltpu.VMEM((halo, 2 * ffn), F32),
                        pltpu.VMEM((tm + halo, tc), F32),
                        pltpu.VMEM((tm, ffn), BF16),
                        pltpu.VMEM((2, tm, d_model), F32),
                        pltpu.SemaphoreType.DMA((2,))],
        compiler_params=pltpu.CompilerParams(
            dimension_semantics=("arbitrary",), vmem_limit_bytes=VMEM_LIMIT),
        name="conv_ffn",
    )(u2, h1, w_up, conv_w, conv_b, w_down)


def _rope_tables(lp, ret_dk, diff_dh):
    pos = jnp.arange(lp, dtype=F32) - PAD
    half = ret_dk // 2
    inv = jnp.power(RET_THETA, -jnp.arange(half, dtype=F32) / half)
    ang = pos[:, None] * inv[None, :]
    rcos, rsin = jnp.cos(ang), jnp.sin(ang)
    rot = diff_dh // 8
    inv = jnp.power(ROPE_THETA, -jnp.arange(rot, dtype=F32) / rot)
    ang = pos[:, None] * inv[None, :]
    cos, sin = jnp.cos(ang), jnp.sin(ang)
    ones = jnp.ones((lp, diff_dh - 2 * rot), F32)
    zeros = jnp.zeros((lp, diff_dh - 2 * rot), F32)
    zr = jnp.zeros((lp, rot), F32)
    rep = LANE // diff_dh
    dc = jnp.tile(jnp.concatenate([cos, cos, ones], axis=1), (1, rep))
    ds1 = jnp.tile(jnp.concatenate([-sin, zr, zeros], axis=1), (1, rep))
    ds2 = jnp.tile(jnp.concatenate([zr, sin, zeros], axis=1), (1, rep))
    return rcos, rsin, (dc, ds1, ds2), rot


def _fold_norm_weight(tables, rot, w, diff_dh, scale):
    dc, ds1, ds2 = tables
    w128 = jnp.tile(w.astype(F32).reshape(-1), LANE // diff_dh) * scale
    return jnp.stack([dc * w128, ds1 * jnp.roll(w128, -rot), ds2 * jnp.roll(w128, rot)])


def kernel(x, meta_tokens, norm1_w, w_in, w_ret_o, q_norm_w, k_norm_w, lambda_q1, lambda_k1,
           lambda_q2, lambda_k2, diff_subln_w, w_diff_o, w_out, norm2_w, w_up, conv_w, conv_b,
           w_down):
    batch, seq, d_model = x.shape
    depth = norm1_w.shape[0]
    lp = CHUNK + seq
    ret_dk = d_model // RET_HEADS
    diff_dv = d_model // DIFF_HEADS
    diff_dh = diff_dv // 2
    assert (lp - CHUNK) % ATT_TQ == 0 and d_model % MXU == 0 and ret_dk // 2 == LANE

    seg_of_tile = ("rq", "rk", "plain", "plain", "silu", "silu", "dq", "dk", "plain",
                   "sigmoid", "sigmoid")
    col_rq, col_rk, col_rv, col_rg = 0, d_model, 2 * d_model, 4 * d_model
    col_dq, col_dk, col_dv, col_gates = 6 * d_model, 7 * d_model, 8 * d_model, 9 * d_model
    assert w_in.shape[2] == len(seg_of_tile) * d_model

    rcos, rsin, diff_tables, rot = _rope_tables(lp, ret_dk, diff_dh)
    gid = np.arange(MXU) // diff_dh
    gmat = jnp.asarray(gid[:, None] == gid[None, :], dtype=BF16)

    meta_chunk = jnp.concatenate([jnp.zeros((PAD, d_model), x.dtype), meta_tokens.astype(x.dtype)], axis=0)

    assert depth == 1, "a second layer would need the pad rows of the residual re-zeroed"
    for l in range(depth):
        lam_init = 0.8 - 0.6 * math.exp(-0.3 * l)
        row = lambda a: a[l].astype(F32).reshape(1, -1)
        qtab = _fold_norm_weight(diff_tables, rot, q_norm_w[l], diff_dh, diff_dh ** -0.5 * LOG2E)
        ktab = _fold_norm_weight(diff_tables, rot, k_norm_w[l], diff_dh, 1.0)
        proj = _input_projection(x, meta_chunk, row(norm1_w), w_in[l].astype(BF16), rcos, rsin,
                                 qtab, ktab, gmat, seg_of_tile, lp, ret_dk, diff_dh)
        ro = _retention(proj, batch, lp, d_model, col_rq, col_rk, col_rv, col_rg)
        score_bound = (jnp.max(jnp.abs(q_norm_w[l].astype(F32))) * jnp.max(jnp.abs(k_norm_w[l].astype(F32)))
                       * (diff_dh ** 0.5 * LOG2E * (1.0 + 2.0 ** -5))).reshape(1)
        do = _diff_attention(proj, score_bound, row(lambda_q1), row(lambda_k1), row(lambda_q2),
                             row(lambda_k2), row(diff_subln_w), batch, lp, d_model,
                             col_dq, col_dk, col_dv, lam_init)
        h1, u2 = _merge(ro, do, proj, x, meta_chunk, w_ret_o[l].astype(BF16),
                        w_diff_o[l].astype(BF16), w_out[l].astype(BF16), row(norm2_w), col_gates, lp)
        out = _ffn(u2, h1, w_up[l].astype(BF16), conv_w[l].astype(F32),
                   conv_b[l].astype(F32).reshape(1, -1), w_down[l].astype(BF16), batch, lp)

    return out
```

```python
import functools
import math

import numpy as np
import jax
import jax.numpy as jnp
from jax import lax
from jax.experimental import pallas as pl
from jax.experimental.pallas import tpu as pltpu

F32 = jnp.float32
BF16 = jnp.bfloat16

N_META = 16
CHUNK = 128
PAD = CHUNK - N_META
RET_HEADS = 4
RET_THETA = 10000.0
DIFF_HEADS = 8
ROPE_THETA = 500000.0
CONV_W = 3
EPS = 1e-6
NEG = -1e30
LANE = 128
MXU = 256
VMEM_LIMIT = 56 * 1024 * 1024

ATT_TQ = 256
ATT_T = 512
ATT_GROUP = 8
LOG2E = math.log2(math.e)


def _pick_tile(n, candidates):
    for c in candidates:
        if n % c == 0:
            return c
    raise ValueError(f"no tile for {n}")


def _dot(a, b):
    return jnp.dot(a, b, preferred_element_type=F32)


def _dot_nt(a, b):
    return lax.dot_general(a, b, (((1,), (1,)), ((), ())), preferred_element_type=F32)


def _sigmoid(v):
    return 0.5 * jnp.tanh(0.5 * v) + 0.5


def _padded_tile_copy(x_hbm, buf_ref, sem, tile, tiles_per_batch, fn):
    tm = buf_ref.shape[0]
    b = lax.div(tile, tiles_per_batch)
    t = lax.rem(tile, tiles_per_batch)

    @pl.when(t == 0)
    def _():
        fn(pltpu.make_async_copy(x_hbm.at[b, pl.ds(0, tm - CHUNK)],
                                 buf_ref.at[pl.ds(CHUNK, tm - CHUNK)], sem))

    @pl.when(t != 0)
    def _():
        start = pl.multiple_of(t * tm - CHUNK, 16)
        fn(pltpu.make_async_copy(x_hbm.at[b, pl.ds(start, tm)], buf_ref, sem))


def _row_tile(lp, limit):
    return max(t for t in range(CHUNK + 16, limit + 1, 16) if lp % t == 0)


def _proj_kernel(seg_of_tile, d_model, ret_dk, diff_dh, ret_k_scale, tiles_per_batch,
                 x_hbm, meta_ref, n1_ref, w_ref, rcos_ref, rsin_ref, qtab_ref, ktab_ref, g_ref,
                 o_ref, u_ref, xbuf_ref, sem_ref):
    i = pl.program_id(0)
    j = pl.program_id(1)
    fetch = functools.partial(_padded_tile_copy, x_hbm, xbuf_ref, sem_ref.at[0],
                              tiles_per_batch=tiles_per_batch)

    @pl.when(j == 0)
    def _():
        @pl.when(i == 0)
        def _():
            fetch(tile=i, fn=lambda c: c.start())

        fetch(tile=i, fn=lambda c: c.wait())

        @pl.when(lax.rem(i, tiles_per_batch) == 0)
        def _():
            xbuf_ref[0:CHUNK, :] = meta_ref[...]

        x = xbuf_ref[...]
        ms = jnp.mean(x * x, axis=-1, keepdims=True)
        u_ref[...] = (x * lax.rsqrt(ms + EPS) * n1_ref[...]).astype(BF16)

        @pl.when(i + 1 < pl.num_programs(0))
        def _():
            fetch(tile=i + 1, fn=lambda c: c.start())

    half = ret_dk // 2

    def project():
        return _dot(u_ref[...], w_ref[...])

    def ret_rope(scale):
        y = project()
        cos = rcos_ref[...]
        sin = rsin_ref[...]
        for h in range(d_model // ret_dk):
            x1 = y[:, h * ret_dk:h * ret_dk + half]
            x2 = y[:, h * ret_dk + half:(h + 1) * ret_dk]
            o_ref[:, h * ret_dk:h * ret_dk + half] = ((x1 * cos - x2 * sin) * scale).astype(BF16)
            o_ref[:, h * ret_dk + half:(h + 1) * ret_dk] = ((x2 * cos + x1 * sin) * scale).astype(BF16)

    def qk_norm_rope(tab_ref):
        y = project()
        sq = (y * y).astype(BF16)
        c = tab_ref[0]
        s1 = tab_ref[1]
        s2 = tab_ref[2]
        rot = diff_dh // 8
        for s in range(d_model // MXU):
            ss = _dot(sq[:, s * MXU:(s + 1) * MXU], g_ref[...])
            inv_rms = lax.rsqrt(ss * (1.0 / diff_dh) + EPS)
            for t in range(MXU // LANE):
                lanes = slice(s * MXU + t * LANE, s * MXU + (t + 1) * LANE)
                x = y[:, lanes]
                z = x * c + pltpu.roll(x, LANE - rot, 1) * s1 + pltpu.roll(x, rot, 1) * s2
                o_ref[:, lanes] = (z * inv_rms[:, t * LANE:(t + 1) * LANE]).astype(BF16)

    def tiles(name):
        ts = [t for t, s in enumerate(seg_of_tile) if s == name]
        cond = j == ts[0]
        for t in ts[1:]:
            cond = cond | (j == t)
        return cond

    @pl.when(tiles("rq"))
    def _():
        ret_rope(1.0)

    @pl.when(tiles("rk"))
    def _():
        ret_rope(ret_k_scale)

    @pl.when(tiles("plain"))
    def _():
        o_ref[...] = project().astype(BF16)

    @pl.when(tiles("silu"))
    def _():
        y = project()
        o_ref[...] = (y * _sigmoid(y)).astype(BF16)

    @pl.when(tiles("sigmoid"))
    def _():
        o_ref[...] = _sigmoid(project()).astype(BF16)

    @pl.when(tiles("dq"))
    def _():
        qk_norm_rope(qtab_ref)

    @pl.when(tiles("dk"))
    def _():
        qk_norm_rope(ktab_ref)


def _input_projection(x, meta_chunk, n1, w_in, rcos, rsin, qtab, ktab, gmat, seg_of_tile, lp,
                      ret_dk, diff_dh):
    batch, _, d_model = x.shape
    rows = batch * lp
    n_col = w_in.shape[1] // d_model
    tm = _row_tile(lp, 1040)
    tpb = lp // tm
    kern = functools.partial(_proj_kernel, seg_of_tile, d_model, ret_dk, diff_dh, ret_dk ** -0.5, tpb)
    tab = pl.BlockSpec((tm, LANE), lambda i, j: (i % tpb, 0))
    tab3 = pl.BlockSpec((3, tm, LANE), lambda i, j: (0, i % tpb, 0))
    vec = pl.BlockSpec((1, d_model), lambda i, j: (0, 0))
    return pl.pallas_call(
        kern,
        grid=(rows // tm, n_col),
        in_specs=[
            pl.BlockSpec(memory_space=pl.ANY),
            pl.BlockSpec((CHUNK, d_model), lambda i, j: (0, 0)),
            vec,
            pl.BlockSpec((d_model, d_model), lambda i, j: (0, j)),
            tab, tab, tab3, tab3,
            pl.BlockSpec((MXU, MXU), lambda i, j: (0, 0)),
        ],
        out_specs=pl.BlockSpec((tm, d_model), lambda i, j: (i, j)),
        out_shape=jax.ShapeDtypeStruct((rows, w_in.shape[1]), BF16),
        scratch_shapes=[pltpu.VMEM((tm, d_model), BF16),
                        pltpu.VMEM((tm, d_model), x.dtype),
                        pltpu.SemaphoreType.DMA((1,))],
        compiler_params=pltpu.CompilerParams(
            dimension_semantics=("arbitrary", "arbitrary"), vmem_limit_bytes=VMEM_LIMIT),
        name="in_proj",
    )(x, meta_chunk, n1, w_in, rcos, rsin, qtab, ktab, gmat)


def _retention_kernel(n_heads, dk, dv, q_ref, k_ref, v_ref, g_ref, intra_ref, qdec_ref,
                      kdec_ref, cdec_ref, o_ref, r_ref):
    c = pl.program_id(1)

    @pl.when(c == 0)
    def _():
        r_ref[...] = jnp.zeros_like(r_ref)

    for cc in range(q_ref.shape[0] // CHUNK):
        rows = slice(cc * CHUNK, (cc + 1) * CHUNK)
        for h in range(n_heads):
            q = q_ref[rows, h * dk:(h + 1) * dk]
            k = k_ref[rows, h * dk:(h + 1) * dk]
            v = v_ref[rows, h * dv:(h + 1) * dv]
            r_old = r_ref[h]
            s = _dot_nt(q, k) * intra_ref[h]
            o = _dot(s.astype(BF16), v) + _dot(q, r_old.astype(BF16)) * qdec_ref[h]
            kd_t = (k.astype(F32) * kdec_ref[h]).T.astype(BF16)
            r_ref[h] = r_old * cdec_ref[h] + _dot(kd_t, v)
            mu = jnp.mean(o, axis=-1, keepdims=True)
            d = o - mu
            var = jnp.mean(d * d, axis=-1, keepdims=True)
            on = d * lax.rsqrt(var + EPS)
            gate = g_ref[rows, h * dv:(h + 1) * dv].astype(F32)
            o_ref[rows, h * dv:(h + 1) * dv] = (gate * on).astype(BF16)


def _retention(proj, batch, lp, d_model, col_q, col_k, col_v, col_g):
    n_heads = RET_HEADS
    dk = d_model // n_heads
    dv = 2 * dk
    n_chunks = lp // CHUNK
    f32 = np.float32
    lg = np.log(f32(1.0) - np.power(f32(2.0), f32(-5.0) - np.arange(n_heads, dtype=f32)))
    idx = np.arange(CHUNK, dtype=f32)
    dist = idx[:, None] - idx[None, :]
    intra = np.where(dist[None] >= 0, np.exp(lg[:, None, None] * np.maximum(dist, f32(0.0))[None]), f32(0.0))
    q_dec = np.exp(lg[:, None] * (idx[None] + f32(1.0)))
    k_dec = np.exp(lg[:, None] * (f32(CHUNK - 1.0) - idx[None]))
    chunk_dec = np.exp(lg * f32(CHUNK))
    intra = intra.astype(f32)
    qdec = np.broadcast_to(q_dec[:, :, None], (n_heads, CHUNK, dv)).astype(f32)
    kdec = np.broadcast_to(k_dec[:, :, None], (n_heads, CHUNK, dk)).astype(f32)
    cdec = np.broadcast_to(chunk_dec[:, None, None], (n_heads, dk, dv)).astype(f32)

    kern = functools.partial(_retention_kernel, n_heads, dk, dv)
    per_step = _pick_tile(n_chunks, (5, 4, 3, 2, 1))
    n_steps = n_chunks // per_step
    tm = per_step * CHUNK
    row = lambda b, c: b * n_steps + c
    const3 = lambda b, c: (0, 0, 0)
    return pl.pallas_call(
        kern,
        grid=(batch, n_steps),
        in_specs=[
            pl.BlockSpec((tm, d_model), lambda b, c: (row(b, c), col_q // d_model)),
            pl.BlockSpec((tm, d_model), lambda b, c: (row(b, c), col_k // d_model)),
            pl.BlockSpec((tm, 2 * d_model), lambda b, c: (row(b, c), col_v // (2 * d_model))),
            pl.BlockSpec((tm, 2 * d_model), lambda b, c: (row(b, c), col_g // (2 * d_model))),
            pl.BlockSpec((n_heads, CHUNK, CHUNK), const3),
            pl.BlockSpec((n_heads, CHUNK, dv), const3),
            pl.BlockSpec((n_heads, CHUNK, dk), const3),
            pl.BlockSpec((n_heads, dk, dv), const3),
        ],
        out_specs=pl.BlockSpec((tm, 2 * d_model), lambda b, c: (row(b, c), 0)),
        out_shape=jax.ShapeDtypeStruct((batch * lp, 2 * d_model), BF16),
        scratch_shapes=[pltpu.VMEM((n_heads, dk, dv), F32)],
        compiler_params=pltpu.CompilerParams(
            dimension_semantics=("arbitrary", "arbitrary"), vmem_limit_bytes=VMEM_LIMIT),
        name="retention",
    )(proj, proj, proj, proj, intra, qdec, kdec, cdec)


def _lambda(lq1_ref, lk1_ref, lq2_ref, lk2_ref, lam_init):
    return (jnp.exp(jnp.sum(lq1_ref[...] * lk1_ref[...], axis=-1, keepdims=True))
            - jnp.exp(jnp.sum(lq2_ref[...] * lk2_ref[...], axis=-1, keepdims=True)) + lam_init)


def _stack_q(q, dh):
    lo = lax.broadcasted_iota(jnp.int32, q.shape, 1) < dh
    zero = jnp.zeros_like(q)
    return jnp.concatenate([jnp.where(lo, q, zero), jnp.where(lo, zero, q)], axis=0)


def _attn_bounded_kernel(lp, dh, lam_init, bound_ref, lq1_ref, lk1_ref, lq2_ref, lk2_ref, sw_ref,
                         q_ref, k_ref, v_ref, o_ref,
                         qs_ref, v1_ref, p0_ref, p1_ref, acc_ref, bias_ref, pt_ref):
    t_blk = ATT_T
    p_refs = (p0_ref, p1_ref)
    dv = 2 * dh
    n_q = lp // t_blk
    tail = lp - n_q * t_blk
    n_stages = n_q * (n_q + 1) // 2
    bound = bound_ref[0]
    lam = _lambda(lq1_ref, lk1_ref, lq2_ref, lk2_ref, lam_init)
    sw = sw_ref[...] * (1.0 - lam_init)

    for blk in range(n_q):
        qs_ref[blk] = _stack_q(q_ref[blk * t_blk:(blk + 1) * t_blk, :], dh)
    v1_ref[0:PAD, :] = jnp.zeros((PAD, 2 * dv), BF16)
    v1_ref[PAD:lp, 0:dv] = v_ref[PAD:lp, :]
    v1_ref[PAD:lp, dv:2 * dv] = jnp.ones((lp - PAD, dv), BF16)

    rd = lax.broadcasted_iota(jnp.int32, (2 * t_blk, t_blk), 0) % t_blk
    cd = lax.broadcasted_iota(jnp.int32, (2 * t_blk, t_blk), 1)
    bias_ref[0] = jnp.full((2 * t_blk, t_blk), bound, F32)
    bias_ref[1] = jnp.where(cd <= rd, bound, -NEG)

    p_refs[1][...] = jnp.zeros_like(p_refs[1])
    acc_ref[...] = jnp.zeros_like(acc_ref)
    n_acc = acc_ref.shape[0]

    def normalize(acc, n):
        r = acc[:, :dv] / jnp.maximum(acc[:, dv:], 1e-30)
        o = r[:n] - lam * r[n:]
        ms = jnp.mean(o * o, axis=-1, keepdims=True)
        return (o * lax.rsqrt(ms + EPS) * sw).astype(BF16)

    def finish(slot):
        return normalize(acc_ref[slot], t_blk)

    def rows(blk):
        return pl.ds(pl.multiple_of(blk * t_blk, t_blk), t_blk)

    def stage(parity, state, scores=True):
        (i, t), (i1, t1) = state
        if scores:
            s = _dot_nt(qs_ref[i], k_ref[rows(t), :])
            p_refs[parity][...] = jnp.exp2(s - bias_ref[(t == i).astype(jnp.int32)]).astype(BF16)
        acc_ref[i1 % n_acc] += _dot(p_refs[1 - parity][...], v1_ref[rows(t1), :])
        wrap = t == i
        return ((jnp.where(wrap, i + 1, i), jnp.where(wrap, 0, t + 1)), (i, t))

    def stage_group(n, state):
        done = []
        for g in range(n):
            done.append(state[1])
            state = stage(g % 2, state)
        for (i1, t1) in done:
            @pl.when((t1 == i1) & (i1 < n_q - 1))
            def _():
                o_ref[rows(i1), :] = finish(i1 % n_acc)
                acc_ref[i1 % n_acc] = jnp.zeros(acc_ref.shape[1:], F32)
        return state

    state = ((jnp.int32(0), jnp.int32(0)), (jnp.int32(1), jnp.int32(0)))
    state = lax.fori_loop(0, n_stages // ATT_GROUP, lambda _, st: stage_group(ATT_GROUP, st), state)
    rest = n_stages % ATT_GROUP
    state = stage_group(rest, state)
    stage(rest % 2, state, scores=False)
    o_ref[(n_q - 1) * t_blk:n_q * t_blk, :] = finish((n_q - 1) % n_acc)

    if tail:
        qt = _stack_q(q_ref[n_q * t_blk:lp, :], dh)
        for blk in range(n_q):
            cols = slice(blk * t_blk, (blk + 1) * t_blk)
            pt_ref[:, cols] = jnp.exp2(_dot_nt(qt, k_ref[cols, :]) - bound).astype(BF16)
        rt = lax.broadcasted_iota(jnp.int32, (2 * tail, tail), 0) % tail
        ct = lax.broadcasted_iota(jnp.int32, (2 * tail, tail), 1)
        s = _dot_nt(qt, k_ref[n_q * t_blk:lp, :])
        pt_ref[:, n_q * t_blk:lp] = jnp.exp2(s - jnp.where(ct <= rt, bound, -NEG)).astype(BF16)
        o_ref[n_q * t_blk:lp, :] = normalize(_dot(pt_ref[...], v1_ref[...]), tail)


def _attn_online_kernel(lp, dh, lam_init, lq1_ref, lk1_ref, lq2_ref, lk2_ref, sw_ref,
                        q_ref, k_ref, v_ref, o_ref):
    tq = ATT_TQ
    n_blocks = (lp - CHUNK) // tq
    lam = _lambda(lq1_ref, lk1_ref, lq2_ref, lk2_ref, lam_init)
    sw = sw_ref[...] * (1.0 - lam_init)
    stack_q = functools.partial(_stack_q, dh=dh)

    def first_block(qs, kb, vb, mask):
        s = jnp.where(mask, _dot_nt(qs, kb), NEG)
        m = jnp.max(s, axis=-1, keepdims=True)
        p = jnp.exp2(s - m)
        l = jnp.sum(p, axis=-1, keepdims=True)
        acc = _dot(p.astype(BF16), vb)
        return m, l, acc

    def next_block(carry, qs, kb, vb, mask):
        m, l, acc = carry
        s = _dot_nt(qs, kb)
        if mask is not None:
            s = jnp.where(mask, s, NEG)
        m_new = jnp.maximum(m, jnp.max(s, axis=-1, keepdims=True))
        alpha = jnp.exp2(m - m_new)
        p = jnp.exp2(s - m_new)
        l = alpha * l + jnp.sum(p, axis=-1, keepdims=True)
        acc = alpha * acc + _dot(p.astype(BF16), vb)
        return m_new, l, acc

    def finish(carry, n):
        _, l, acc = carry
        o = acc[:n] / l[:n] - lam * (acc[n:] / l[n:])
        ms = jnp.mean(o * o, axis=-1, keepdims=True)
        return (o * lax.rsqrt(ms + EPS) * sw).astype(BF16)

    r0 = lax.broadcasted_iota(jnp.int32, (2 * CHUNK, CHUNK), 0) % CHUNK
    c0 = lax.broadcasted_iota(jnp.int32, (2 * CHUNK, CHUNK), 1)
    k0 = k_ref[0:CHUNK, :]
    v0 = v_ref[0:CHUNK, :]
    carry = first_block(stack_q(q_ref[0:CHUNK, :]), k0, v0, (c0 <= r0) & (c0 >= PAD))
    o_ref[0:CHUNK, :] = finish(carry, CHUNK)

    meta_mask = lax.broadcasted_iota(jnp.int32, (2 * tq, CHUNK), 1) >= PAD
    rd = lax.broadcasted_iota(jnp.int32, (2 * tq, tq), 0) % tq
    cd = lax.broadcasted_iota(jnp.int32, (2 * tq, tq), 1)
    diag_mask = cd <= rd

    def q_block(i, _):
        row = pl.multiple_of(CHUNK + i * tq, CHUNK)
        qs = stack_q(q_ref[pl.ds(row, tq), :])
        carry = first_block(qs, k0, v0, meta_mask)

        def k_block(jb, carry):
            kr = pl.multiple_of(CHUNK + jb * tq, CHUNK)
            return next_block(carry, qs, k_ref[pl.ds(kr, tq), :], v_ref[pl.ds(kr, tq), :], None)

        carry = lax.fori_loop(0, i, k_block, carry)
        carry = next_block(carry, qs, k_ref[pl.ds(row, tq), :], v_ref[pl.ds(row, tq), :], diag_mask)
        o_ref[pl.ds(row, tq), :] = finish(carry, tq)
        return 0

    lax.fori_loop(0, n_blocks, q_block, 0)


MAX_SAFE_SCORE_BOUND = 60.0


def _diff_attention(proj, score_bound, lq1, lk1, lq2, lk2, subw, batch, lp, d_model,
                    col_q, col_k, col_v, lam_init):
    dv = d_model // DIFF_HEADS
    dh = dv // 2
    proj3 = proj.reshape(batch, lp, proj.shape[1])
    assert lp >= ATT_T and lp % ATT_T, "the kernel expects full blocks plus a short last block"
    small = lambda n: pl.BlockSpec((1, n), lambda b, h: (0, 0))
    head = lambda col: pl.BlockSpec((None, lp, dv), lambda b, h: (b, 0, col // dv + h))
    common = dict(
        grid=(batch, DIFF_HEADS),
        out_specs=pl.BlockSpec((None, lp, dv), lambda b, h: (b, 0, h)),
        out_shape=jax.ShapeDtypeStruct((batch, lp, d_model), BF16),
        compiler_params=pltpu.CompilerParams(
            dimension_semantics=("arbitrary", "arbitrary"), vmem_limit_bytes=VMEM_LIMIT),
    )
    tensors = [small(dh), small(dh), small(dh), small(dh), small(dv),
               head(col_q), head(col_k), head(col_v)]

    def bounded():
        return pl.pallas_call(
            functools.partial(_attn_bounded_kernel, lp, dh, lam_init),
            in_specs=[pl.BlockSpec(memory_space=pltpu.SMEM)] + tensors,
            scratch_shapes=[pltpu.VMEM((lp // ATT_T, 2 * ATT_T, dv), BF16),
                            pltpu.VMEM((lp, 2 * dv), BF16),
                            pltpu.VMEM((2 * ATT_T, ATT_T), BF16),
                            pltpu.VMEM((2 * ATT_T, ATT_T), BF16),
                            pltpu.VMEM((ATT_GROUP, 2 * ATT_T, 2 * dv), F32),
                            pltpu.VMEM((2, 2 * ATT_T, ATT_T), F32),
                            pltpu.VMEM((2 * (lp % ATT_T), lp), BF16)],
            name="diff_attn", **common,
        )(score_bound, lq1, lk1, lq2, lk2, subw, proj3, proj3, proj3)

    def online():
        return pl.pallas_call(
            functools.partial(_attn_online_kernel, lp, dh, lam_init),
            in_specs=tensors, name="diff_attn_online", **common,
        )(lq1, lk1, lq2, lk2, subw, proj3, proj3, proj3)

    out = lax.cond(score_bound[0] <= MAX_SAFE_SCORE_BOUND, bounded, online)
    return out.reshape(batch * lp, d_model)


def _merge_kernel(tiles_per_batch, ro_ref, do_ref, gr_ref, gd_ref, x_hbm, meta_ref, wr_ref, wd_ref,
                  wo_ref, n2_ref, h1_ref, u2_ref, hbuf_ref, sem_ref):
    step = pl.program_id(0)

    def fetch(tile, fn):
        slot = lax.rem(tile, 2)
        _padded_tile_copy(x_hbm, hbuf_ref.at[slot], sem_ref.at[slot], tile, tiles_per_batch, fn)

    @pl.when(step == 0)
    def _():
        fetch(step, lambda c: c.start())

    @pl.when(step + 1 < pl.num_programs(0))
    def _():
        fetch(step + 1, lambda c: c.start())

    ro = _dot(ro_ref[...], wr_ref[...])
    do = _dot(do_ref[...], wd_ref[...])
    z = gr_ref[...].astype(F32) * ro + gd_ref[...].astype(F32) * do
    mix = _dot(z.astype(BF16), wo_ref[...])

    fetch(step, lambda c: c.wait())
    slot = lax.rem(step, 2)

    @pl.when(lax.rem(step, tiles_per_batch) == 0)
    def _():
        hbuf_ref[slot, 0:CHUNK, :] = meta_ref[...]

    h1 = hbuf_ref[slot] + mix
    h1_ref[...] = h1
    ms = jnp.mean(h1 * h1, axis=-1, keepdims=True)
    u2_ref[...] = (h1 * lax.rsqrt(ms + EPS) * n2_ref[...]).astype(BF16)


def _merge(ro, do, proj, x, meta_chunk, w_ret_o, w_diff_o, w_out, n2, col_gates, lp):
    batch, _, d_model = x.shape
    rows = batch * lp
    tm = _row_tile(lp, 640)
    gcol = col_gates // d_model
    const = lambda i: (0, 0)
    return pl.pallas_call(
        functools.partial(_merge_kernel, lp // tm),
        grid=(rows // tm,),
        in_specs=[
            pl.BlockSpec((tm, ro.shape[1]), lambda i: (i, 0)),
            pl.BlockSpec((tm, d_model), lambda i: (i, 0)),
            pl.BlockSpec((tm, d_model), lambda i: (i, gcol)),
            pl.BlockSpec((tm, d_model), lambda i: (i, gcol + 1)),
            pl.BlockSpec(memory_space=pl.ANY),
            pl.BlockSpec((CHUNK, d_model), const),
            pl.BlockSpec(w_ret_o.shape, const),
            pl.BlockSpec(w_diff_o.shape, const),
            pl.BlockSpec(w_out.shape, const),
            pl.BlockSpec((1, d_model), const),
        ],
        out_specs=[pl.BlockSpec((tm, d_model), lambda i: (i, 0)),
                   pl.BlockSpec((tm, d_model), lambda i: (i, 0))],
        out_shape=[jax.ShapeDtypeStruct((rows, d_model), F32),
                   jax.ShapeDtypeStruct((rows, d_model), BF16)],
        scratch_shapes=[pltpu.VMEM((2, tm, d_model), x.dtype),
                        pltpu.SemaphoreType.DMA((2,))],
        compiler_params=pltpu.CompilerParams(
            dimension_semantics=("arbitrary",), vmem_limit_bytes=VMEM_LIMIT),
        name="merge",
    )(ro, do, proj, proj, x, meta_chunk, w_ret_o, w_diff_o, w_out, n2)


def _ffn_kernel(ffn, tc, tiles_per_batch, u_ref, h_ref, wu_ref, cw_ref, cb_ref, wd_ref, o_hbm,
                carry_ref, up_ref, act_ref, obuf_ref, sem_ref):
    tm = u_ref.shape[0]
    halo = carry_ref.shape[0]
    step = pl.program_id(0)
    n_steps = pl.num_programs(0)

    def out_copy(s, first):
        slot = lax.rem(s, 2)
        b = lax.div(s, tiles_per_batch)
        if first:
            return pltpu.make_async_copy(obuf_ref.at[slot, pl.ds(CHUNK, tm - CHUNK)],
                                         o_hbm.at[b, pl.ds(0, tm - CHUNK)], sem_ref.at[slot])
        start = pl.multiple_of(lax.rem(s, tiles_per_batch) * tm - CHUNK, CHUNK)
        return pltpu.make_async_copy(obuf_ref.at[slot], o_hbm.at[b, pl.ds(start, tm)], sem_ref.at[slot])

    def for_copy(s, fn):
        first = lax.rem(s, tiles_per_batch) == 0

        @pl.when(first)
        def _():
            fn(out_copy(s, True))

        @pl.when(jnp.logical_not(first))
        def _():
            fn(out_copy(s, False))

    @pl.when(step == 0)
    def _():
        carry_ref[...] = jnp.zeros_like(carry_ref)

    @pl.when(step >= 2)
    def _():
        for_copy(step - 2, lambda c: c.wait())

    u = u_ref[...]

    def conv_cols(col):
        up_ref[0:halo, :] = carry_ref[:, col:col + tc]
        up_ref[halo:halo + tm, :] = _dot(u, wu_ref[:, col:col + tc])
        carry_ref[:, col:col + tc] = up_ref[tm:tm + halo, :]
        y = cb_ref[:, col:col + tc]
        for t in range(CONV_W):
            lo = halo - (CONV_W - 1) + t
            y = y + up_ref[lo:lo + tm, :] * cw_ref[t:t + 1, col:col + tc]
        return y

    for c in range(ffn // tc):
        a = conv_cols(c * tc)
        b = conv_cols(ffn + c * tc)
        act_ref[:, c * tc:(c + 1) * tc] = (a * _sigmoid(a) * b).astype(BF16)

    obuf_ref[lax.rem(step, 2)] = h_ref[...] + _dot(act_ref[...], wd_ref[...])
    for_copy(step, lambda c: c.start())

    @pl.when(step == n_steps - 1)
    def _():
        @pl.when(step >= 1)
        def _():
            for_copy(step - 1, lambda c: c.wait())

        for_copy(step, lambda c: c.wait())


def _ffn(u2, h1, w_up, conv_w, conv_b, w_down, batch, lp):
    rows, d_model = h1.shape
    ffn = w_down.shape[0]
    tc = MXU
    halo = 8
    tm = _row_tile(lp, 640)
    const = lambda i: (0, 0)
    resident = lambda shape: pl.BlockSpec(shape, const, pipeline_mode=pl.Buffered(1))
    kern = functools.partial(_ffn_kernel, ffn, tc, lp // tm)
    return pl.pallas_call(
        kern,
        grid=(rows // tm,),
        in_specs=[
            pl.BlockSpec((tm, d_model), lambda i: (i, 0)),
            pl.BlockSpec((tm, d_model), lambda i: (i, 0)),
            resident(w_up.shape),
            pl.BlockSpec(conv_w.shape, const),
            pl.BlockSpec(conv_b.shape, const),
            resident(w_down.shape),
        ],
        out_specs=pl.BlockSpec(memory_space=pl.ANY),
        out_shape=jax.ShapeDtypeStruct((batch, lp - CHUNK, d_model), F32),
        scratch_shapes=[pltpu.VMEM((halo, 2 * ffn), F32),
                        pltpu.VMEM((tm + halo, tc), F32),
                        pltpu.VMEM((tm, ffn), BF16),
                        pltpu.VMEM((2, tm, d_model), F32),
                        pltpu.SemaphoreType.DMA((2,))],
        compiler_params=pltpu.CompilerParams(
            dimension_semantics=("arbitrary",), vmem_limit_bytes=VMEM_LIMIT),
        name="conv_ffn",
    )(u2, h1, w_up, conv_w, conv_b, w_down)


def _rope_tables(lp, ret_dk, diff_dh):
    f32 = np.float32
    pos = np.arange(lp, dtype=f32) - f32(PAD)
    half = ret_dk // 2
    inv = np.power(f32(RET_THETA), -np.arange(half, dtype=f32) / f32(half))
    ang = pos[:, None] * inv[None, :]
    rcos, rsin = np.cos(ang), np.sin(ang)
    rot = diff_dh // 8
    inv = np.power(f32(ROPE_THETA), -np.arange(rot, dtype=f32) / f32(rot))
    ang = pos[:, None] * inv[None, :]
    cos, sin = np.cos(ang), np.sin(ang)
    ones = np.ones((lp, diff_dh - 2 * rot), f32)
    zeros = np.zeros((lp, diff_dh - 2 * rot), f32)
    zr = np.zeros((lp, rot), f32)
    rep = LANE // diff_dh
    dc = np.tile(np.concatenate([cos, cos, ones], axis=1), (1, rep))
    ds1 = np.tile(np.concatenate([-sin, zr, zeros], axis=1), (1, rep))
    ds2 = np.tile(np.concatenate([zr, sin, zeros], axis=1), (1, rep))
    return rcos, rsin, (dc, ds1, ds2), rot


def _fold_norm_weight(tables, rot, w, diff_dh, scale):
    dc, ds1, ds2 = tables
    w128 = jnp.tile(w.astype(F32).reshape(-1), LANE // diff_dh) * scale
    return jnp.stack([dc * w128, ds1 * jnp.roll(w128, -rot), ds2 * jnp.roll(w128, rot)])


def kernel(x, meta_tokens, norm1_w, w_in, w_ret_o, q_norm_w, k_norm_w, lambda_q1, lambda_k1,
           lambda_q2, lambda_k2, diff_subln_w, w_diff_o, w_out, norm2_w, w_up, conv_w, conv_b,
           w_down):
    batch, seq, d_model = x.shape
    depth = norm1_w.shape[0]
    lp = CHUNK + seq
    ret_dk = d_model // RET_HEADS
    diff_dv = d_model // DIFF_HEADS
    diff_dh = diff_dv // 2
    assert (lp - CHUNK) % ATT_TQ == 0 and d_model % MXU == 0 and ret_dk // 2 == LANE

    seg_of_tile = ("rq", "rk", "plain", "plain", "silu", "silu", "dq", "dk", "plain",
                   "sigmoid", "sigmoid")
    col_rq, col_rk, col_rv, col_rg = 0, d_model, 2 * d_model, 4 * d_model
    col_dq, col_dk, col_dv, col_gates = 6 * d_model, 7 * d_model, 8 * d_model, 9 * d_model
    assert w_in.shape[2] == len(seg_of_tile) * d_model

    rcos, rsin, diff_tables, rot = _rope_tables(lp, ret_dk, diff_dh)
    gid = np.arange(MXU) // diff_dh
    gmat = jnp.asarray(gid[:, None] == gid[None, :], dtype=BF16)

    meta_chunk = jnp.concatenate([jnp.zeros((PAD, d_model), x.dtype), meta_tokens.astype(x.dtype)], axis=0)

    assert depth == 1, "a second layer would need the pad rows of the residual re-zeroed"
    for l in range(depth):
        lam_init = 0.8 - 0.6 * math.exp(-0.3 * l)
        row = lambda a: a[l].astype(F32).reshape(1, -1)
        qtab = _fold_norm_weight(diff_tables, rot, q_norm_w[l], diff_dh, diff_dh ** -0.5 * LOG2E)
        ktab = _fold_norm_weight(diff_tables, rot, k_norm_w[l], diff_dh, 1.0)
        proj = _input_projection(x, meta_chunk, row(norm1_w), w_in[l].astype(BF16), rcos, rsin,
                                 qtab, ktab, gmat, seg_of_tile, lp, ret_dk, diff_dh)
        ro = _retention(proj, batch, lp, d_model, col_rq, col_rk, col_rv, col_rg)
        score_bound = (jnp.max(jnp.abs(q_norm_w[l].astype(F32))) * jnp.max(jnp.abs(k_norm_w[l].astype(F32)))
                       * (diff_dh ** 0.5 * LOG2E * (1.0 + 2.0 ** -5))).reshape(1)
        do = _diff_attention(proj, score_bound, row(lambda_q1), row(lambda_k1), row(lambda_q2),
                             row(lambda_k2), row(diff_subln_w), batch, lp, d_model,
                             col_dq, col_dk, col_dv, lam_init)
        h1, u2 = _merge(ro, do, proj, x, meta_chunk, w_ret_o[l].astype(BF16),
                        w_diff_o[l].astype(BF16), w_out[l].astype(BF16), row(norm2_w), col_gates, lp)
        out = _ffn(u2, h1, w_up[l].astype(BF16), conv_w[l].astype(F32),
                   conv_b[l].astype(F32).reshape(1, -1), w_down[l].astype(BF16), batch, lp)

    return out
```

```python
import functools
import math

import numpy as np
import jax
import jax.numpy as jnp
from jax import lax
from jax.experimental import pallas as pl
from jax.experimental.pallas import tpu as pltpu

F32 = jnp.float32
BF16 = jnp.bfloat16

N_META = 16
CHUNK = 128
PAD = CHUNK - N_META
RET_HEADS = 4
RET_THETA = 10000.0
DIFF_HEADS = 8
ROPE_THETA = 500000.0
CONV_W = 3
EPS = 1e-6
NEG = -1e30
LANE = 128
MXU = 256
VMEM_LIMIT = 56 * 1024 * 1024

ATT_TQ = 256
ATT_T = 512
ATT_GROUP = 16
ATT_ACC_SLOTS = 8
LOG2E = math.log2(math.e)


def _pick_tile(n, candidates):
    for c in candidates:
        if n % c == 0:
            return c
    raise ValueError(f"no tile for {n}")


def _dot(a, b):
    return jnp.dot(a, b, preferred_element_type=F32)


def _dot_nt(a, b):
    return lax.dot_general(a, b, (((1,), (1,)), ((), ())), preferred_element_type=F32)


def _sigmoid(v):
    return 0.5 * jnp.tanh(0.5 * v) + 0.5


def _padded_tile_copy(x_hbm, buf_ref, sem, tile, tiles_per_batch, fn):
    tm = buf_ref.shape[0]
    b = lax.div(tile, tiles_per_batch)
    t = lax.rem(tile, tiles_per_batch)

    @pl.when(t == 0)
    def _():
        fn(pltpu.make_async_copy(x_hbm.at[b, pl.ds(0, tm - CHUNK)],
                                 buf_ref.at[pl.ds(CHUNK, tm - CHUNK)], sem))

    @pl.when(t != 0)
    def _():
        start = pl.multiple_of(t * tm - CHUNK, 16)
        fn(pltpu.make_async_copy(x_hbm.at[b, pl.ds(start, tm)], buf_ref, sem))


def _row_tile(lp, limit):
    return max(t for t in range(CHUNK + 16, limit + 1, 16) if lp % t == 0)


def _proj_kernel(seg_of_tile, d_model, ret_dk, diff_dh, ret_k_scale, tiles_per_batch,
                 x_hbm, meta_ref, n1_ref, w_ref, rcos_ref, rsin_ref, qtab_ref, ktab_ref, g_ref,
                 o_ref, u_ref, xbuf_ref, sem_ref):
    i = pl.program_id(0)
    j = pl.program_id(1)
    fetch = functools.partial(_padded_tile_copy, x_hbm, xbuf_ref, sem_ref.at[0],
                              tiles_per_batch=tiles_per_batch)

    @pl.when(j == 0)
    def _():
        @pl.when(i == 0)
        def _():
            fetch(tile=i, fn=lambda c: c.start())

        fetch(tile=i, fn=lambda c: c.wait())

        @pl.when(lax.rem(i, tiles_per_batch) == 0)
        def _():
            xbuf_ref[0:CHUNK, :] = meta_ref[...]

        x = xbuf_ref[...]
        ms = jnp.mean(x * x, axis=-1, keepdims=True)
        u_ref[...] = (x * lax.rsqrt(ms + EPS) * n1_ref[...]).astype(BF16)

        @pl.when(i + 1 < pl.num_programs(0))
        def _():
            fetch(tile=i + 1, fn=lambda c: c.start())

    half = ret_dk // 2

    def project():
        return _dot(u_ref[...], w_ref[...])

    def ret_rope(scale):
        y = project()
        cos = rcos_ref[...]
        sin = rsin_ref[...]
        for h in range(d_model // ret_dk):
            x1 = y[:, h * ret_dk:h * ret_dk + half]
            x2 = y[:, h * ret_dk + half:(h + 1) * ret_dk]
            o_ref[:, h * ret_dk:h * ret_dk + half] = ((x1 * cos - x2 * sin) * scale).astype(BF16)
            o_ref[:, h * ret_dk + half:(h + 1) * ret_dk] = ((x2 * cos + x1 * sin) * scale).astype(BF16)

    def qk_norm_rope(tab_ref):
        y = project()
        sq = (y * y).astype(BF16)
        c = tab_ref[0]
        s1 = tab_ref[1]
        s2 = tab_ref[2]
        rot = diff_dh // 8
        for s in range(d_model // MXU):
            ss = _dot(sq[:, s * MXU:(s + 1) * MXU], g_ref[...])
            inv_rms = lax.rsqrt(ss * (1.0 / diff_dh) + EPS)
            for t in range(MXU // LANE):
                lanes = slice(s * MXU + t * LANE, s * MXU + (t + 1) * LANE)
                x = y[:, lanes]
                z = x * c + pltpu.roll(x, LANE - rot, 1) * s1 + pltpu.roll(x, rot, 1) * s2
                o_ref[:, lanes] = (z * inv_rms[:, t * LANE:(t + 1) * LANE]).astype(BF16)

    def tiles(name):
        ts = [t for t, s in enumerate(seg_of_tile) if s == name]
        cond = j == ts[0]
        for t in ts[1:]:
            cond = cond | (j == t)
        return cond

    @pl.when(tiles("rq"))
    def _():
        ret_rope(1.0)

    @pl.when(tiles("rk"))
    def _():
        ret_rope(ret_k_scale)

    @pl.when(tiles("plain"))
    def _():
        o_ref[...] = project().astype(BF16)

    @pl.when(tiles("silu"))
    def _():
        y = project()
        o_ref[...] = (y * _sigmoid(y)).astype(BF16)

    @pl.when(tiles("sigmoid"))
    def _():
        o_ref[...] = _sigmoid(project()).astype(BF16)

    @pl.when(tiles("dq"))
    def _():
        qk_norm_rope(qtab_ref)

    @pl.when(tiles("dk"))
    def _():
        qk_norm_rope(ktab_ref)


def _input_projection(x, meta_chunk, n1, w_in, rcos, rsin, qtab, ktab, gmat, seg_of_tile, lp,
                      ret_dk, diff_dh):
    batch, _, d_model = x.shape
    rows = batch * lp
    n_col = w_in.shape[1] // d_model
    tm = _row_tile(lp, 1040)
    tpb = lp // tm
    kern = functools.partial(_proj_kernel, seg_of_tile, d_model, ret_dk, diff_dh, ret_dk ** -0.5, tpb)
    tab = pl.BlockSpec((tm, LANE), lambda i, j: (i % tpb, 0))
    tab3 = pl.BlockSpec((3, tm, LANE), lambda i, j: (0, i % tpb, 0))
    vec = pl.BlockSpec((1, d_model), lambda i, j: (0, 0))
    return pl.pallas_call(
        kern,
        grid=(rows // tm, n_col),
        in_specs=[
            pl.BlockSpec(memory_space=pl.ANY),
            pl.BlockSpec((CHUNK, d_model), lambda i, j: (0, 0)),
            vec,
            pl.BlockSpec((d_model, d_model), lambda i, j: (0, j)),
            tab, tab, tab3, tab3,
            pl.BlockSpec((MXU, MXU), lambda i, j: (0, 0)),
        ],
        out_specs=pl.BlockSpec((tm, d_model), lambda i, j: (i, j)),
        out_shape=jax.ShapeDtypeStruct((rows, w_in.shape[1]), BF16),
        scratch_shapes=[pltpu.VMEM((tm, d_model), BF16),
                        pltpu.VMEM((tm, d_model), x.dtype),
                        pltpu.SemaphoreType.DMA((1,))],
        compiler_params=pltpu.CompilerParams(
            dimension_semantics=("arbitrary", "arbitrary"), vmem_limit_bytes=VMEM_LIMIT),
        name="in_proj",
    )(x, meta_chunk, n1, w_in, rcos, rsin, qtab, ktab, gmat)


def _retention_kernel(n_heads, dk, dv, q_ref, k_ref, v_ref, g_ref, intra_ref, qdec_ref,
                      kdec_ref, cdec_ref, o_ref, r_ref):
    c = pl.program_id(1)

    @pl.when(c == 0)
    def _():
        r_ref[...] = jnp.zeros_like(r_ref)

    for cc in range(q_ref.shape[0] // CHUNK):
        rows = slice(cc * CHUNK, (cc + 1) * CHUNK)
        for h in range(n_heads):
            q = q_ref[rows, h * dk:(h + 1) * dk]
            k = k_ref[rows, h * dk:(h + 1) * dk]
            v = v_ref[rows, h * dv:(h + 1) * dv]
            r_old = r_ref[h]
            s = _dot_nt(q, k) * intra_ref[h]
            o = _dot(s.astype(BF16), v) + _dot(q, r_old.astype(BF16)) * qdec_ref[h]
            kd_t = (k.astype(F32) * kdec_ref[h]).T.astype(BF16)
            r_ref[h] = r_old * cdec_ref[h] + _dot(kd_t, v)
            mu = jnp.mean(o, axis=-1, keepdims=True)
            d = o - mu
            var = jnp.mean(d * d, axis=-1, keepdims=True)
            on = d * lax.rsqrt(var + EPS)
            gate = g_ref[rows, h * dv:(h + 1) * dv].astype(F32)
            o_ref[rows, h * dv:(h + 1) * dv] = (gate * on).astype(BF16)


def _retention(proj, batch, lp, d_model, col_q, col_k, col_v, col_g):
    n_heads = RET_HEADS
    dk = d_model // n_heads
    dv = 2 * dk
    n_chunks = lp // CHUNK
    f32 = np.float32
    lg = np.log(f32(1.0) - np.power(f32(2.0), f32(-5.0) - np.arange(n_heads, dtype=f32)))
    idx = np.arange(CHUNK, dtype=f32)
    dist = idx[:, None] - idx[None, :]
    intra = np.where(dist[None] >= 0, np.exp(lg[:, None, None] * np.maximum(dist, f32(0.0))[None]), f32(0.0))
    q_dec = np.exp(lg[:, None] * (idx[None] + f32(1.0)))
    k_dec = np.exp(lg[:, None] * (f32(CHUNK - 1.0) - idx[None]))
    chunk_dec = np.exp(lg * f32(CHUNK))
    intra = intra.astype(f32)
    qdec = np.broadcast_to(q_dec[:, :, None], (n_heads, CHUNK, dv)).astype(f32)
    kdec = np.broadcast_to(k_dec[:, :, None], (n_heads, CHUNK, dk)).astype(f32)
    cdec = np.broadcast_to(chunk_dec[:, None, None], (n_heads, dk, dv)).astype(f32)

    kern = functools.partial(_retention_kernel, n_heads, dk, dv)
    per_step = _pick_tile(n_chunks, (5, 4, 3, 2, 1))
    n_steps = n_chunks // per_step
    tm = per_step * CHUNK
    row = lambda b, c: b * n_steps + c
    const3 = lambda b, c: (0, 0, 0)
    return pl.pallas_call(
        kern,
        grid=(batch, n_steps),
        in_specs=[
            pl.BlockSpec((tm, d_model), lambda b, c: (row(b, c), col_q // d_model)),
            pl.BlockSpec((tm, d_model), lambda b, c: (row(b, c), col_k // d_model)),
            pl.BlockSpec((tm, 2 * d_model), lambda b, c: (row(b, c), col_v // (2 * d_model))),
            pl.BlockSpec((tm, 2 * d_model), lambda b, c: (row(b, c), col_g // (2 * d_model))),
            pl.BlockSpec((n_heads, CHUNK, CHUNK), const3),
            pl.BlockSpec((n_heads, CHUNK, dv), const3),
            pl.BlockSpec((n_heads, CHUNK, dk), const3),
            pl.BlockSpec((n_heads, dk, dv), const3),
        ],
        out_specs=pl.BlockSpec((tm, 2 * d_model), lambda b, c: (row(b, c), 0)),
        out_shape=jax.ShapeDtypeStruct((batch * lp, 2 * d_model), BF16),
        scratch_shapes=[pltpu.VMEM((n_heads, dk, dv), F32)],
        compiler_params=pltpu.CompilerParams(
            dimension_semantics=("arbitrary", "arbitrary"), vmem_limit_bytes=VMEM_LIMIT),
        name="retention",
    )(proj, proj, proj, proj, intra, qdec, kdec, cdec)


def _lambda(lq1_ref, lk1_ref, lq2_ref, lk2_ref, lam_init):
    return (jnp.exp(jnp.sum(lq1_ref[...] * lk1_ref[...], axis=-1, keepdims=True))
            - jnp.exp(jnp.sum(lq2_ref[...] * lk2_ref[...], axis=-1, keepdims=True)) + lam_init)


def _stack_q(q, dh):
    lo = lax.broadcasted_iota(jnp.int32, q.shape, 1) < dh
    zero = jnp.zeros_like(q)
    return jnp.concatenate([jnp.where(lo, q, zero), jnp.where(lo, zero, q)], axis=0)


def _attn_bounded_kernel(lp, dh, lam_init, bound_ref, lq1_ref, lk1_ref, lq2_ref, lk2_ref, sw_ref,
                         q_ref, k_ref, v_ref, o_ref,
                         qs_ref, v1_ref, p0_ref, p1_ref, acc_ref, bias_ref, pt_ref):
    t_blk = ATT_T
    p_refs = (p0_ref, p1_ref)
    dv = 2 * dh
    n_q = lp // t_blk
    tail = lp - n_q * t_blk
    n_stages = n_q * (n_q + 1) // 2
    bound = bound_ref[0]
    lam = _lambda(lq1_ref, lk1_ref, lq2_ref, lk2_ref, lam_init)
    sw = sw_ref[...] * (1.0 - lam_init)

    for blk in range(n_q):
        qs_ref[blk] = _stack_q(q_ref[blk * t_blk:(blk + 1) * t_blk, :], dh)
    v1_ref[0:PAD, :] = jnp.zeros((PAD, 2 * dv), BF16)
    v1_ref[PAD:lp, 0:dv] = v_ref[PAD:lp, :]
    v1_ref[PAD:lp, dv:2 * dv] = jnp.ones((lp - PAD, dv), BF16)

    rd = lax.broadcasted_iota(jnp.int32, (2 * t_blk, t_blk), 0) % t_blk
    cd = lax.broadcasted_iota(jnp.int32, (2 * t_blk, t_blk), 1)
    bias_ref[0] = jnp.full((2 * t_blk, t_blk), bound, F32)
    bias_ref[1] = jnp.where(cd <= rd, bound, -NEG)

    p_refs[1][...] = jnp.zeros_like(p_refs[1])
    acc_ref[...] = jnp.zeros_like(acc_ref)
    n_acc = acc_ref.shape[0]

    def normalize(acc, n):
        r = acc[:, :dv] / jnp.maximum(acc[:, dv:], 1e-30)
        o = r[:n] - lam * r[n:]
        ms = jnp.mean(o * o, axis=-1, keepdims=True)
        return (o * lax.rsqrt(ms + EPS) * sw).astype(BF16)

    def finish(slot):
        return normalize(acc_ref[slot], t_blk)

    def rows(blk):
        return pl.ds(pl.multiple_of(blk * t_blk, t_blk), t_blk)

    def stage(parity, state, scores=True):
        (i, t), (i1, t1) = state
        if scores:
            s = _dot_nt(qs_ref[i], k_ref[rows(t), :])
            p_refs[parity][...] = jnp.exp2(s - bias_ref[(t == i).astype(jnp.int32)]).astype(BF16)
        acc_ref[i1 % n_acc] += _dot(p_refs[1 - parity][...], v1_ref[rows(t1), :])
        wrap = t == i
        return ((jnp.where(wrap, i + 1, i), jnp.where(wrap, 0, t + 1)), (i, t))

    def stage_group(n, state):
        assert n <= 4 * n_acc
        done = []
        for g in range(n):
            done.append(state[1])
            state = stage(g % 2, state)
        for (i1, t1) in done:
            @pl.when((t1 == i1) & (i1 < n_q - 1))
            def _():
                o_ref[rows(i1), :] = finish(i1 % n_acc)
                acc_ref[i1 % n_acc] = jnp.zeros(acc_ref.shape[1:], F32)
        return state

    state = ((jnp.int32(0), jnp.int32(0)), (jnp.int32(1), jnp.int32(0)))
    state = lax.fori_loop(0, n_stages // ATT_GROUP, lambda _, st: stage_group(ATT_GROUP, st), state)
    rest = n_stages % ATT_GROUP
    state = stage_group(rest, state)
    stage(rest % 2, state, scores=False)
    o_ref[(n_q - 1) * t_blk:n_q * t_blk, :] = finish((n_q - 1) % n_acc)

    if tail:
        qt = _stack_q(q_ref[n_q * t_blk:lp, :], dh)
        for blk in range(n_q):
            cols = slice(blk * t_blk, (blk + 1) * t_blk)
            pt_ref[:, cols] = jnp.exp2(_dot_nt(qt, k_ref[cols, :]) - bound).astype(BF16)
        rt = lax.broadcasted_iota(jnp.int32, (2 * tail, tail), 0) % tail
        ct = lax.broadcasted_iota(jnp.int32, (2 * tail, tail), 1)
        s = _dot_nt(qt, k_ref[n_q * t_blk:lp, :])
        pt_ref[:, n_q * t_blk:lp] = jnp.exp2(s - jnp.where(ct <= rt, bound, -NEG)).astype(BF16)
        o_ref[n_q * t_blk:lp, :] = normalize(_dot(pt_ref[...], v1_ref[...]), tail)


def _attn_online_kernel(lp, dh, lam_init, lq1_ref, lk1_ref, lq2_ref, lk2_ref, sw_ref,
                        q_ref, k_ref, v_ref, o_ref):
    tq = ATT_TQ
    n_blocks = (lp - CHUNK) // tq
    lam = _lambda(lq1_ref, lk1_ref, lq2_ref, lk2_ref, lam_init)
    sw = sw_ref[...] * (1.0 - lam_init)
    stack_q = functools.partial(_stack_q, dh=dh)

    def first_block(qs, kb, vb, mask):
        s = jnp.where(mask, _dot_nt(qs, kb), NEG)
        m = jnp.max(s, axis=-1, keepdims=True)
        p = jnp.exp2(s - m)
        l = jnp.sum(p, axis=-1, keepdims=True)
        acc = _dot(p.astype(BF16), vb)
        return m, l, acc

    def next_block(carry, qs, kb, vb, mask):
        m, l, acc = carry
        s = _dot_nt(qs, kb)
        if mask is not None:
            s = jnp.where(mask, s, NEG)
        m_new = jnp.maximum(m, jnp.max(s, axis=-1, keepdims=True))
        alpha = jnp.exp2(m - m_new)
        p = jnp.exp2(s - m_new)
        l = alpha * l + jnp.sum(p, axis=-1, keepdims=True)
        acc = alpha * acc + _dot(p.astype(BF16), vb)
        return m_new, l, acc

    def finish(carry, n):
        _, l, acc = carry
        o = acc[:n] / l[:n] - lam * (acc[n:] / l[n:])
        ms = jnp.mean(o * o, axis=-1, keepdims=True)
        return (o * lax.rsqrt(ms + EPS) * sw).astype(BF16)

    r0 = lax.broadcasted_iota(jnp.int32, (2 * CHUNK, CHUNK), 0) % CHUNK
    c0 = lax.broadcasted_iota(jnp.int32, (2 * CHUNK, CHUNK), 1)
    k0 = k_ref[0:CHUNK, :]
    v0 = v_ref[0:CHUNK, :]
    carry = first_block(stack_q(q_ref[0:CHUNK, :]), k0, v0, (c0 <= r0) & (c0 >= PAD))
    o_ref[0:CHUNK, :] = finish(carry, CHUNK)

    meta_mask = lax.broadcasted_iota(jnp.int32, (2 * tq, CHUNK), 1) >= PAD
    rd = lax.broadcasted_iota(jnp.int32, (2 * tq, tq), 0) % tq
    cd = lax.broadcasted_iota(jnp.int32, (2 * tq, tq), 1)
    diag_mask = cd <= rd

    def q_block(i, _):
        row = pl.multiple_of(CHUNK + i * tq, CHUNK)
        qs = stack_q(q_ref[pl.ds(row, tq), :])
        carry = first_block(qs, k0, v0, meta_mask)

        def k_block(jb, carry):
            kr = pl.multiple_of(CHUNK + jb * tq, CHUNK)
            return next_block(carry, qs, k_ref[pl.ds(kr, tq), :], v_ref[pl.ds(kr, tq), :], None)

        carry = lax.fori_loop(0, i, k_block, carry)
        carry = next_block(carry, qs, k_ref[pl.ds(row, tq), :], v_ref[pl.ds(row, tq), :], diag_mask)
        o_ref[pl.ds(row, tq), :] = finish(carry, tq)
        return 0

    lax.fori_loop(0, n_blocks, q_block, 0)


MAX_SAFE_SCORE_BOUND = 60.0


def _diff_attention(proj, score_bound, lq1, lk1, lq2, lk2, subw, batch, lp, d_model,
                    col_q, col_k, col_v, lam_init):
    dv = d_model // DIFF_HEADS
    dh = dv // 2
    proj3 = proj.reshape(batch, lp, proj.shape[1])
    assert lp >= ATT_T and lp % ATT_T, "the kernel expects full blocks plus a short last block"
    small = lambda n: pl.BlockSpec((1, n), lambda b, h: (0, 0))
    head = lambda col: pl.BlockSpec((None, lp, dv), lambda b, h: (b, 0, col // dv + h))
    common = dict(
        grid=(batch, DIFF_HEADS),
        out_specs=pl.BlockSpec((None, lp, dv), lambda b, h: (b, 0, h)),
        out_shape=jax.ShapeDtypeStruct((batch, lp, d_model), BF16),
        compiler_params=pltpu.CompilerParams(
            dimension_semantics=("arbitrary", "arbitrary"), vmem_limit_bytes=VMEM_LIMIT),
    )
    tensors = [small(dh), small(dh), small(dh), small(dh), small(dv),
               head(col_q), head(col_k), head(col_v)]

    def bounded():
        return pl.pallas_call(
            functools.partial(_attn_bounded_kernel, lp, dh, lam_init),
            in_specs=[pl.BlockSpec(memory_space=pltpu.SMEM)] + tensors,
            scratch_shapes=[pltpu.VMEM((lp // ATT_T, 2 * ATT_T, dv), BF16),
                            pltpu.VMEM((lp, 2 * dv), BF16),
                            pltpu.VMEM((2 * ATT_T, ATT_T), BF16),
                            pltpu.VMEM((2 * ATT_T, ATT_T), BF16),
                            pltpu.VMEM((ATT_ACC_SLOTS, 2 * ATT_T, 2 * dv), F32),
                            pltpu.VMEM((2, 2 * ATT_T, ATT_T), F32),
                            pltpu.VMEM((2 * (lp % ATT_T), lp), BF16)],
            name="diff_attn", **common,
        )(score_bound, lq1, lk1, lq2, lk2, subw, proj3, proj3, proj3)

    def online():
        return pl.pallas_call(
            functools.partial(_attn_online_kernel, lp, dh, lam_init),
            in_specs=tensors, name="diff_attn_online", **common,
        )(lq1, lk1, lq2, lk2, subw, proj3, proj3, proj3)

    out = lax.cond(score_bound[0] <= MAX_SAFE_SCORE_BOUND, bounded, online)
    return out.reshape(batch * lp, d_model)


def _merge_kernel(tiles_per_batch, ro_ref, do_ref, gr_ref, gd_ref, x_hbm, meta_ref, wr_ref, wd_ref,
                  wo_ref, n2_ref, h1_ref, u2_ref, hbuf_ref, sem_ref):
    step = pl.program_id(0)

    def fetch(tile, fn):
        slot = lax.rem(tile, 2)
        _padded_tile_copy(x_hbm, hbuf_ref.at[slot], sem_ref.at[slot], tile, tiles_per_batch, fn)

    @pl.when(step == 0)
    def _():
        fetch(step, lambda c: c.start())

    @pl.when(step + 1 < pl.num_programs(0))
    def _():
        fetch(step + 1, lambda c: c.start())

    ro = _dot(ro_ref[...], wr_ref[...])
    do = _dot(do_ref[...], wd_ref[...])
    z = gr_ref[...].astype(F32) * ro + gd_ref[...].astype(F32) * do
    mix = _dot(z.astype(BF16), wo_ref[...])

    fetch(step, lambda c: c.wait())
    slot = lax.rem(step, 2)

    @pl.when(lax.rem(step, tiles_per_batch) == 0)
    def _():
        hbuf_ref[slot, 0:CHUNK, :] = meta_ref[...]

    h1 = hbuf_ref[slot] + mix
    h1_ref[...] = h1
    ms = jnp.mean(h1 * h1, axis=-1, keepdims=True)
    u2_ref[...] = (h1 * lax.rsqrt(ms + EPS) * n2_ref[...]).astype(BF16)


def _merge(ro, do, proj, x, meta_chunk, w_ret_o, w_diff_o, w_out, n2, col_gates, lp):
    batch, _, d_model = x.shape
    rows = batch * lp
    tm = _row_tile(lp, 640)
    gcol = col_gates // d_model
    const = lambda i: (0, 0)
    return pl.pallas_call(
        functools.partial(_merge_kernel, lp // tm),
        grid=(rows // tm,),
        in_specs=[
            pl.BlockSpec((tm, ro.shape[1]), lambda i: (i, 0)),
            pl.BlockSpec((tm, d_model), lambda i: (i, 0)),
            pl.BlockSpec((tm, d_model), lambda i: (i, gcol)),
            pl.BlockSpec((tm, d_model), lambda i: (i, gcol + 1)),
            pl.BlockSpec(memory_space=pl.ANY),
            pl.BlockSpec((CHUNK, d_model), const),
            pl.BlockSpec(w_ret_o.shape, const),
            pl.BlockSpec(w_diff_o.shape, const),
            pl.BlockSpec(w_out.shape, const),
            pl.BlockSpec((1, d_model), const),
        ],
        out_specs=[pl.BlockSpec((tm, d_model), lambda i: (i, 0)),
                   pl.BlockSpec((tm, d_model), lambda i: (i, 0))],
        out_shape=[jax.ShapeDtypeStruct((rows, d_model), F32),
                   jax.ShapeDtypeStruct((rows, d_model), BF16)],
        scratch_shapes=[pltpu.VMEM((2, tm, d_model), x.dtype),
                        pltpu.SemaphoreType.DMA((2,))],
        compiler_params=pltpu.CompilerParams(
            dimension_semantics=("arbitrary",), vmem_limit_bytes=VMEM_LIMIT),
        name="merge",
    )(ro, do, proj, proj, x, meta_chunk, w_ret_o, w_diff_o, w_out, n2)


def _ffn_kernel(ffn, tc, tiles_per_batch, u_ref, h_ref, wu_ref, cw_ref, cb_ref, wd_ref, o_hbm,
                carry_ref, up_ref, act_ref, obuf_ref, sem_ref):
    tm = u_ref.shape[0]
    halo = carry_ref.shape[0]
    step = pl.program_id(0)
    n_steps = pl.num_programs(0)

    def out_copy(s, first):
        slot = lax.rem(s, 2)
        b = lax.div(s, tiles_per_batch)
        if first:
            return pltpu.make_async_copy(obuf_ref.at[slot, pl.ds(CHUNK, tm - CHUNK)],
                                         o_hbm.at[b, pl.ds(0, tm - CHUNK)], sem_ref.at[slot])
        start = pl.multiple_of(lax.rem(s, tiles_per_batch) * tm - CHUNK, CHUNK)
        return pltpu.make_async_copy(obuf_ref.at[slot], o_hbm.at[b, pl.ds(start, tm)], sem_ref.at[slot])

    def for_copy(s, fn):
        first = lax.rem(s, tiles_per_batch) == 0

        @pl.when(first)
        def _():
            fn(out_copy(s, True))

        @pl.when(jnp.logical_not(first))
        def _():
            fn(out_copy(s, False))

    @pl.when(step == 0)
    def _():
        carry_ref[...] = jnp.zeros_like(carry_ref)

    @pl.when(step >= 2)
    def _():
        for_copy(step - 2, lambda c: c.wait())

    u = u_ref[...]

    def conv_cols(col):
        up_ref[0:halo, :] = carry_ref[:, col:col + tc]
        up_ref[halo:halo + tm, :] = _dot(u, wu_ref[:, col:col + tc])
        carry_ref[:, col:col + tc] = up_ref[tm:tm + halo, :]
        y = cb_ref[:, col:col + tc]
        for t in range(CONV_W):
            lo = halo - (CONV_W - 1) + t
            y = y + up_ref[lo:lo + tm, :] * cw_ref[t:t + 1, col:col + tc]
        return y

    for c in range(ffn // tc):
        a = conv_cols(c * tc)
        b = conv_cols(ffn + c * tc)
        act_ref[:, c * tc:(c + 1) * tc] = (a * _sigmoid(a) * b).astype(BF16)

    obuf_ref[lax.rem(step, 2)] = h_ref[...] + _dot(act_ref[...], wd_ref[...])
    for_copy(step, lambda c: c.start())

    @pl.when(step == n_steps - 1)
    def _():
        @pl.when(step >= 1)
        def _():
            for_copy(step - 1, lambda c: c.wait())

        for_copy(step, lambda c: c.wait())


def _ffn(u2, h1, w_up, conv_w, conv_b, w_down, batch, lp):
    rows, d_model = h1.shape
    ffn = w_down.shape[0]
    tc = MXU
    halo = 8
    tm = _row_tile(lp, 640)
    const = lambda i: (0, 0)
    resident = lambda shape: pl.BlockSpec(shape, const, pipeline_mode=pl.Buffered(1))
    kern = functools.partial(_ffn_kernel, ffn, tc, lp // tm)
    return pl.pallas_call(
        kern,
        grid=(rows // tm,),
        in_specs=[
            pl.BlockSpec((tm, d_model), lambda i: (i, 0)),
            pl.BlockSpec((tm, d_model), lambda i: (i, 0)),
            resident(w_up.shape),
            pl.BlockSpec(conv_w.shape, const),
            pl.BlockSpec(conv_b.shape, const),
            resident(w_down.shape),
        ],
        out_specs=pl.BlockSpec(memory_space=pl.ANY),
        out_shape=jax.ShapeDtypeStruct((batch, lp - CHUNK, d_model), F32),
        scratch_shapes=[pltpu.VMEM((halo, 2 * ffn), F32),
                        pltpu.VMEM((tm + halo, tc), F32),
                        pltpu.VMEM((tm, ffn), BF16),
                        pltpu.VMEM((2, tm, d_model), F32),
                        pltpu.SemaphoreType.DMA((2,))],
        compiler_params=pltpu.CompilerParams(
            dimension_semantics=("arbitrary",), vmem_limit_bytes=VMEM_LIMIT),
        name="conv_ffn",
    )(u2, h1, w_up, conv_w, conv_b, w_down)


def _rope_tables(lp, ret_dk, diff_dh):
    f32 = np.float32
    pos = np.arange(lp, dtype=f32) - f32(PAD)
    half = ret_dk // 2
    inv = np.power(f32(RET_THETA), -np.arange(half, dtype=f32) / f32(half))
    ang = pos[:, None] * inv[None, :]
    rcos, rsin = np.cos(ang), np.sin(ang)
    rot = diff_dh // 8
    inv = np.power(f32(ROPE_THETA), -np.arange(rot, dtype=f32) / f32(rot))
    ang = pos[:, None] * inv[None, :]
    cos, sin = np.cos(ang), np.sin(ang)
    ones = np.ones((lp, diff_dh - 2 * rot), f32)
    zeros = np.zeros((lp, diff_dh - 2 * rot), f32)
    zr = np.zeros((lp, rot), f32)
    rep = LANE // diff_dh
    dc = np.tile(np.concatenate([cos, cos, ones], axis=1), (1, rep))
    ds1 = np.tile(np.concatenate([-sin, zr, zeros], axis=1), (1, rep))
    ds2 = np.tile(np.concatenate([zr, sin, zeros], axis=1), (1, rep))
    return rcos, rsin, (dc, ds1, ds2), rot


def _fold_norm_weight(tables, rot, w, diff_dh, scale):
    dc, ds1, ds2 = tables
    w128 = jnp.tile(w.astype(F32).reshape(-1), LANE // diff_dh) * scale
    return jnp.stack([dc * w128, ds1 * jnp.roll(w128, -rot), ds2 * jnp.roll(w128, rot)])


def kernel(x, meta_tokens, norm1_w, w_in, w_ret_o, q_norm_w, k_norm_w, lambda_q1, lambda_k1,
           lambda_q2, lambda_k2, diff_subln_w, w_diff_o, w_out, norm2_w, w_up, conv_w, conv_b,
           w_down):
    batch, seq, d_model = x.shape
    depth = norm1_w.shape[0]
    lp = CHUNK + seq
    ret_dk = d_model // RET_HEADS
    diff_dv = d_model // DIFF_HEADS
    diff_dh = diff_dv // 2
    assert (lp - CHUNK) % ATT_TQ == 0 and d_model % MXU == 0 and ret_dk // 2 == LANE

    seg_of_tile = ("rq", "rk", "plain", "plain", "silu", "silu", "dq", "dk", "plain",
                   "sigmoid", "sigmoid")
    col_rq, col_rk, col_rv, col_rg = 0, d_model, 2 * d_model, 4 * d_model
    col_dq, col_dk, col_dv, col_gates = 6 * d_model, 7 * d_model, 8 * d_model, 9 * d_model
    assert w_in.shape[2] == len(seg_of_tile) * d_model

    rcos, rsin, diff_tables, rot = _rope_tables(lp, ret_dk, diff_dh)
    gid = np.arange(MXU) // diff_dh
    gmat = jnp.asarray(gid[:, None] == gid[None, :], dtype=BF16)

    meta_chunk = jnp.concatenate([jnp.zeros((PAD, d_model), x.dtype), meta_tokens.astype(x.dtype)], axis=0)

    assert depth == 1, "a second layer would need the pad rows of the residual re-zeroed"
    for l in range(depth):
        lam_init = 0.8 - 0.6 * math.exp(-0.3 * l)
        row = lambda a: a[l].astype(F32).reshape(1, -1)
        qtab = _fold_norm_weight(diff_tables, rot, q_norm_w[l], diff_dh, diff_dh ** -0.5 * LOG2E)
        ktab = _fold_norm_weight(diff_tables, rot, k_norm_w[l], diff_dh, 1.0)
        proj = _input_projection(x, meta_chunk, row(norm1_w), w_in[l].astype(BF16), rcos, rsin,
                                 qtab, ktab, gmat, seg_of_tile, lp, ret_dk, diff_dh)
        ro = _retention(proj, batch, lp, d_model, col_rq, col_rk, col_rv, col_rg)
        score_bound = (jnp.max(jnp.abs(q_norm_w[l].astype(F32))) * jnp.max(jnp.abs(k_norm_w[l].astype(F32)))
                       * (diff_dh ** 0.5 * LOG2E * (1.0 + 2.0 ** -5))).reshape(1)
        do = _diff_attention(proj, score_bound, row(lambda_q1), row(lambda_k1), row(lambda_q2),
                             row(lambda_k2), row(diff_subln_w), batch, lp, d_model,
                             col_dq, col_dk, col_dv, lam_init)
        h1, u2 = _merge(ro, do, proj, x, meta_chunk, w_ret_o[l].astype(BF16),
                        w_diff_o[l].astype(BF16), w_out[l].astype(BF16), row(norm2_w), col_gates, lp)
        out = _ffn(u2, h1, w_up[l].astype(BF16), conv_w[l].astype(F32),
                   conv_b[l].astype(F32).reshape(1, -1), w_down[l].astype(BF16), batch, lp)

    return out
```

```python
import functools
import math

import numpy as np
import jax
import jax.numpy as jnp
from jax import lax
from jax.experimental import pallas as pl
from jax.experimental.pallas import tpu as pltpu

F32 = jnp.float32
BF16 = jnp.bfloat16

N_META = 16
CHUNK = 128
PAD = CHUNK - N_META
RET_HEADS = 4
RET_THETA = 10000.0
DIFF_HEADS = 8
ROPE_THETA = 500000.0
CONV_W = 3
EPS = 1e-6
NEG = -1e30
LANE = 128
MXU = 256
VMEM_LIMIT = 56 * 1024 * 1024

ATT_TQ = 256
ATT_T = 512
ATT_GROUP = 8
LOG2E = math.log2(math.e)


def _pick_tile(n, candidates):
    for c in candidates:
        if n % c == 0:
            return c
    raise ValueError(f"no tile for {n}")


def _dot(a, b):
    return jnp.dot(a, b, preferred_element_type=F32)


def _dot_nt(a, b):
    return lax.dot_general(a, b, (((1,), (1,)), ((), ())), preferred_element_type=F32)


def _sigmoid(v):
    return 0.5 * jnp.tanh(0.5 * v) + 0.5


def _padded_tile_copy(x_hbm, buf_ref, sem, tile, tiles_per_batch, fn):
    tm = buf_ref.shape[0]
    b = lax.div(tile, tiles_per_batch)
    t = lax.rem(tile, tiles_per_batch)

    @pl.when(t == 0)
    def _():
        fn(pltpu.make_async_copy(x_hbm.at[b, pl.ds(0, tm - CHUNK)],
                                 buf_ref.at[pl.ds(CHUNK, tm - CHUNK)], sem))

    @pl.when(t != 0)
    def _():
        start = pl.multiple_of(t * tm - CHUNK, 16)
        fn(pltpu.make_async_copy(x_hbm.at[b, pl.ds(start, tm)], buf_ref, sem))


def _row_tile(lp, limit):
    return max(t for t in range(CHUNK + 16, limit + 1, 16) if lp % t == 0)


def _proj_kernel(seg_of_tile, d_model, ret_dk, diff_dh, ret_k_scale, tiles_per_batch,
                 x_hbm, meta_ref, n1_ref, w_ref, rcos_ref, rsin_ref, qtab_ref, ktab_ref, g_ref,
                 o_ref, u_ref, xbuf_ref, sem_ref):
    i = pl.program_id(0)
    j = pl.program_id(1)
    fetch = functools.partial(_padded_tile_copy, x_hbm, xbuf_ref, sem_ref.at[0],
                              tiles_per_batch=tiles_per_batch)

    @pl.when(j == 0)
    def _():
        @pl.when(i == 0)
        def _():
            fetch(tile=i, fn=lambda c: c.start())

        fetch(tile=i, fn=lambda c: c.wait())

        @pl.when(lax.rem(i, tiles_per_batch) == 0)
        def _():
            xbuf_ref[0:CHUNK, :] = meta_ref[...]

        x = xbuf_ref[...]
        ms = jnp.mean(x * x, axis=-1, keepdims=True)
        u_ref[...] = (x * lax.rsqrt(ms + EPS) * n1_ref[...]).astype(BF16)

        @pl.when(i + 1 < pl.num_programs(0))
        def _():
            fetch(tile=i + 1, fn=lambda c: c.start())

    half = ret_dk // 2

    def project():
        return _dot(u_ref[...], w_ref[...])

    def ret_rope(scale):
        y = project()
        cos = rcos_ref[...]
        sin = rsin_ref[...]
        for h in range(d_model // ret_dk):
            x1 = y[:, h * ret_dk:h * ret_dk + half]
            x2 = y[:, h * ret_dk + half:(h + 1) * ret_dk]
            o_ref[:, h * ret_dk:h * ret_dk + half] = ((x1 * cos - x2 * sin) * scale).astype(BF16)
            o_ref[:, h * ret_dk + half:(h + 1) * ret_dk] = ((x2 * cos + x1 * sin) * scale).astype(BF16)

    def qk_norm_rope(tab_ref):
        y = project()
        sq = (y * y).astype(BF16)
        c = tab_ref[0]
        s1 = tab_ref[1]
        s2 = tab_ref[2]
        rot = diff_dh // 8
        for s in range(d_model // MXU):
            ss = _dot(sq[:, s * MXU:(s + 1) * MXU], g_ref[...])
            inv_rms = lax.rsqrt(ss * (1.0 / diff_dh) + EPS)
            for t in range(MXU // LANE):
                lanes = slice(s * MXU + t * LANE, s * MXU + (t + 1) * LANE)
                x = y[:, lanes]
                z = x * c + pltpu.roll(x, LANE - rot, 1) * s1 + pltpu.roll(x, rot, 1) * s2
                o_ref[:, lanes] = (z * inv_rms[:, t * LANE:(t + 1) * LANE]).astype(BF16)

    def tiles(name):
        ts = [t for t, s in enumerate(seg_of_tile) if s == name]
        cond = j == ts[0]
        for t in ts[1:]:
            cond = cond | (j == t)
        return cond

    @pl.when(tiles("rq"))
    def _():
        ret_rope(1.0)

    @pl.when(tiles("rk"))
    def _():
        ret_rope(ret_k_scale)

    @pl.when(tiles("plain"))
    def _():
        o_ref[...] = project().astype(BF16)

    @pl.when(tiles("silu"))
    def _():
        y = project()
        o_ref[...] = (y * _sigmoid(y)).astype(BF16)

    @pl.when(tiles("sigmoid"))
    def _():
        o_ref[...] = _sigmoid(project()).astype(BF16)

    @pl.when(tiles("dq"))
    def _():
        qk_norm_rope(qtab_ref)

    @pl.when(tiles("dk"))
    def _():
        qk_norm_rope(ktab_ref)


def _input_projection(x, meta_chunk, n1, w_in, rcos, rsin, qtab, ktab, gmat, seg_of_tile, lp,
                      ret_dk, diff_dh):
    batch, _, d_model = x.shape
    rows = batch * lp
    n_col = w_in.shape[1] // d_model
    tm = _row_tile(lp, 1040)
    tpb = lp // tm
    kern = functools.partial(_proj_kernel, seg_of_tile, d_model, ret_dk, diff_dh, ret_dk ** -0.5, tpb)
    tab = pl.BlockSpec((tm, LANE), lambda i, j: (i % tpb, 0))
    tab3 = pl.BlockSpec((3, tm, LANE), lambda i, j: (0, i % tpb, 0))
    vec = pl.BlockSpec((1, d_model), lambda i, j: (0, 0))
    return pl.pallas_call(
        kern,
        grid=(rows // tm, n_col),
        in_specs=[
            pl.BlockSpec(memory_space=pl.ANY),
            pl.BlockSpec((CHUNK, d_model), lambda i, j: (0, 0)),
            vec,
            pl.BlockSpec((d_model, d_model), lambda i, j: (0, j)),
            tab, tab, tab3, tab3,
            pl.BlockSpec((MXU, MXU), lambda i, j: (0, 0)),
        ],
        out_specs=pl.BlockSpec((tm, d_model), lambda i, j: (i, j)),
        out_shape=jax.ShapeDtypeStruct((rows, w_in.shape[1]), BF16),
        scratch_shapes=[pltpu.VMEM((tm, d_model), BF16),
                        pltpu.VMEM((tm, d_model), x.dtype),
                        pltpu.SemaphoreType.DMA((1,))],
        compiler_params=pltpu.CompilerParams(
            dimension_semantics=("arbitrary", "arbitrary"), vmem_limit_bytes=VMEM_LIMIT),
        name="in_proj",
    )(x, meta_chunk, n1, w_in, rcos, rsin, qtab, ktab, gmat)


def _retention_kernel(n_heads, dk, dv, q_ref, k_ref, v_ref, g_ref, intra_ref, qdec_ref,
                      kdec_ref, cdec_ref, o_ref, r_ref):
    c = pl.program_id(1)

    @pl.when(c == 0)
    def _():
        r_ref[...] = jnp.zeros_like(r_ref)

    for cc in range(q_ref.shape[0] // CHUNK):
        rows = slice(cc * CHUNK, (cc + 1) * CHUNK)
        for h in range(n_heads):
            q = q_ref[rows, h * dk:(h + 1) * dk]
            k = k_ref[rows, h * dk:(h + 1) * dk]
            v = v_ref[rows, h * dv:(h + 1) * dv]
            r_old = r_ref[h]
            s = _dot_nt(q, k) * intra_ref[h]
            o = _dot(s.astype(BF16), v) + _dot(q, r_old.astype(BF16)) * qdec_ref[h]
            kd_t = (k.astype(F32) * kdec_ref[h]).T.astype(BF16)
            r_ref[h] = r_old * cdec_ref[h] + _dot(kd_t, v)
            mu = jnp.mean(o, axis=-1, keepdims=True)
            d = o - mu
            var = jnp.mean(d * d, axis=-1, keepdims=True)
            on = d * lax.rsqrt(var + EPS)
            gate = g_ref[rows, h * dv:(h + 1) * dv].astype(F32)
            o_ref[rows, h * dv:(h + 1) * dv] = (gate * on).astype(BF16)


def _retention(proj, batch, lp, d_model, col_q, col_k, col_v, col_g):
    n_heads = RET_HEADS
    dk = d_model // n_heads
    dv = 2 * dk
    n_chunks = lp // CHUNK
    f32 = np.float32
    lg = np.log(f32(1.0) - np.power(f32(2.0), f32(-5.0) - np.arange(n_heads, dtype=f32)))
    idx = np.arange(CHUNK, dtype=f32)
    dist = idx[:, None] - idx[None, :]
    intra = np.where(dist[None] >= 0, np.exp(lg[:, None, None] * np.maximum(dist, f32(0.0))[None]), f32(0.0))
    q_dec = np.exp(lg[:, None] * (idx[None] + f32(1.0)))
    k_dec = np.exp(lg[:, None] * (f32(CHUNK - 1.0) - idx[None]))
    chunk_dec = np.exp(lg * f32(CHUNK))
    intra = intra.astype(f32)
    qdec = np.broadcast_to(q_dec[:, :, None], (n_heads, CHUNK, dv)).astype(f32)
    kdec = np.broadcast_to(k_dec[:, :, None], (n_heads, CHUNK, dk)).astype(f32)
    cdec = np.broadcast_to(chunk_dec[:, None, None], (n_heads, dk, dv)).astype(f32)

    kern = functools.partial(_retention_kernel, n_heads, dk, dv)
    per_step = _pick_tile(n_chunks, (5, 4, 3, 2, 1))
    n_steps = n_chunks // per_step
    tm = per_step * CHUNK
    row = lambda b, c: b * n_steps + c
    const3 = lambda b, c: (0, 0, 0)
    return pl.pallas_call(
        kern,
        grid=(batch, n_steps),
        in_specs=[
            pl.BlockSpec((tm, d_model), lambda b, c: (row(b, c), col_q // d_model)),
            pl.BlockSpec((tm, d_model), lambda b, c: (row(b, c), col_k // d_model)),
            pl.BlockSpec((tm, 2 * d_model), lambda b, c: (row(b, c), col_v // (2 * d_model))),
            pl.BlockSpec((tm, 2 * d_model), lambda b, c: (row(b, c), col_g // (2 * d_model))),
            pl.BlockSpec((n_heads, CHUNK, CHUNK), const3),
            pl.BlockSpec((n_heads, CHUNK, dv), const3),
            pl.BlockSpec((n_heads, CHUNK, dk), const3),
            pl.BlockSpec((n_heads, dk, dv), const3),
        ],
        out_specs=pl.BlockSpec((tm, 2 * d_model), lambda b, c: (row(b, c), 0)),
        out_shape=jax.ShapeDtypeStruct((batch * lp, 2 * d_model), BF16),
        scratch_shapes=[pltpu.VMEM((n_heads, dk, dv), F32)],
        compiler_params=pltpu.CompilerParams(
            dimension_semantics=("arbitrary", "arbitrary"), vmem_limit_bytes=VMEM_LIMIT),
        name="retention",
    )(proj, proj, proj, proj, intra, qdec, kdec, cdec)


def _lambda(lq1_ref, lk1_ref, lq2_ref, lk2_ref, lam_init):
    return (jnp.exp(jnp.sum(lq1_ref[...] * lk1_ref[...], axis=-1, keepdims=True))
            - jnp.exp(jnp.sum(lq2_ref[...] * lk2_ref[...], axis=-1, keepdims=True)) + lam_init)


def _stack_q(q, dh):
    lo = lax.broadcasted_iota(jnp.int32, q.shape, 1) < dh
    zero = jnp.zeros_like(q)
    return jnp.concatenate([jnp.where(lo, q, zero), jnp.where(lo, zero, q)], axis=0)


def _attn_bounded_kernel(lp, dh, lam_init, bound_ref, lq1_ref, lk1_ref, lq2_ref, lk2_ref, sw_ref,
                         q_ref, k_ref, v_ref, o_ref,
                         qs_ref, v1_ref, p0_ref, p1_ref, acc_ref, bias_ref, pt_ref):
    t_blk = ATT_T
    p_refs = (p0_ref, p1_ref)
    dv = 2 * dh
    n_q = lp // t_blk
    tail = lp - n_q * t_blk
    n_stages = n_q * (n_q + 1) // 2
    bound = bound_ref[0]
    lam = _lambda(lq1_ref, lk1_ref, lq2_ref, lk2_ref, lam_init)
    sw = sw_ref[...] * (1.0 - lam_init)

    for blk in range(n_q):
        qs_ref[blk] = _stack_q(q_ref[blk * t_blk:(blk + 1) * t_blk, :], dh)
    v1_ref[0:PAD, :] = jnp.zeros((PAD, 2 * dv), BF16)
    v1_ref[PAD:lp, 0:dv] = v_ref[PAD:lp, :]
    v1_ref[PAD:lp, dv:2 * dv] = jnp.ones((lp - PAD, dv), BF16)

    rd = lax.broadcasted_iota(jnp.int32, (2 * t_blk, t_blk), 0) % t_blk
    cd = lax.broadcasted_iota(jnp.int32, (2 * t_blk, t_blk), 1)
    bias_ref[0] = jnp.full((2 * t_blk, t_blk), bound, F32)
    bias_ref[1] = jnp.where(cd <= rd, bound, -NEG)

    p_refs[1][...] = jnp.zeros_like(p_refs[1])
    acc_ref[...] = jnp.zeros_like(acc_ref)
    n_acc = acc_ref.shape[0]

    def normalize(acc, n):
        r = acc[:, :dv] / jnp.maximum(acc[:, dv:], 1e-30)
        o = r[:n] - lam * r[n:]
        ms = jnp.mean(o * o, axis=-1, keepdims=True)
        return (o * lax.rsqrt(ms + EPS) * sw).astype(BF16)

    def finish(slot):
        return normalize(acc_ref[slot], t_blk)

    def rows(blk):
        return pl.ds(pl.multiple_of(blk * t_blk, t_blk), t_blk)

    def stage(parity, state, scores=True):
        (i, t), (i1, t1) = state
        if scores:
            s = _dot_nt(qs_ref[i], k_ref[rows(t), :])
            p_refs[parity][...] = jnp.exp2(s - bias_ref[(t == i).astype(jnp.int32)]).astype(BF16)
        acc_ref[i1 % n_acc] += _dot(p_refs[1 - parity][...], v1_ref[rows(t1), :])
        wrap = t == i
        return ((jnp.where(wrap, i + 1, i), jnp.where(wrap, 0, t + 1)), (i, t))

    def stage_group(n, state):
        done = []
        for g in range(n):
            done.append(state[1])
            state = stage(g % 2, state)
        for (i1, t1) in done:
            @pl.when((t1 == i1) & (i1 < n_q - 1))
            def _():
                o_ref[rows(i1), :] = finish(i1 % n_acc)
                acc_ref[i1 % n_acc] = jnp.zeros(acc_ref.shape[1:], F32)
        return state

    state = ((jnp.int32(0), jnp.int32(0)), (jnp.int32(1), jnp.int32(0)))
    state = lax.fori_loop(0, n_stages // ATT_GROUP, lambda _, st: stage_group(ATT_GROUP, st), state)
    rest = n_stages % ATT_GROUP
    state = stage_group(rest, state)
    stage(rest % 2, state, scores=False)
    o_ref[(n_q - 1) * t_blk:n_q * t_blk, :] = finish((n_q - 1) % n_acc)

    if tail:
        qt = _stack_q(q_ref[n_q * t_blk:lp, :], dh)
        for blk in range(n_q):
            cols = slice(blk * t_blk, (blk + 1) * t_blk)
            pt_ref[:, cols] = jnp.exp2(_dot_nt(qt, k_ref[cols, :]) - bound).astype(BF16)
        rt = lax.broadcasted_iota(jnp.int32, (2 * tail, tail), 0) % tail
        ct = lax.broadcasted_iota(jnp.int32, (2 * tail, tail), 1)
        s = _dot_nt(qt, k_ref[n_q * t_blk:lp, :])
        pt_ref[:, n_q * t_blk:lp] = jnp.exp2(s - jnp.where(ct <= rt, bound, -NEG)).astype(BF16)
        o_ref[n_q * t_blk:lp, :] = normalize(_dot(pt_ref[...], v1_ref[...]), tail)


def _attn_online_kernel(lp, dh, lam_init, lq1_ref, lk1_ref, lq2_ref, lk2_ref, sw_ref,
                        q_ref, k_ref, v_ref, o_ref):
    tq = ATT_TQ
    n_blocks = (lp - CHUNK) // tq
    lam = _lambda(lq1_ref, lk1_ref, lq2_ref, lk2_ref, lam_init)
    sw = sw_ref[...] * (1.0 - lam_init)
    stack_q = functools.partial(_stack_q, dh=dh)

    def first_block(qs, kb, vb, mask):
        s = jnp.where(mask, _dot_nt(qs, kb), NEG)
        m = jnp.max(s, axis=-1, keepdims=True)
        p = jnp.exp2(s - m)
        l = jnp.sum(p, axis=-1, keepdims=True)
        acc = _dot(p.astype(BF16), vb)
        return m, l, acc

    def next_block(carry, qs, kb, vb, mask):
        m, l, acc = carry
        s = _dot_nt(qs, kb)
        if mask is not None:
            s = jnp.where(mask, s, NEG)
        m_new = jnp.maximum(m, jnp.max(s, axis=-1, keepdims=True))
        alpha = jnp.exp2(m - m_new)
        p = jnp.exp2(s - m_new)
        l = alpha * l + jnp.sum(p, axis=-1, keepdims=True)
        acc = alpha * acc + _dot(p.astype(BF16), vb)
        return m_new, l, acc

    def finish(carry, n):
        _, l, acc = carry
        o = acc[:n] / l[:n] - lam * (acc[n:] / l[n:])
        ms = jnp.mean(o * o, axis=-1, keepdims=True)
        return (o * lax.rsqrt(ms + EPS) * sw).astype(BF16)

    r0 = lax.broadcasted_iota(jnp.int32, (2 * CHUNK, CHUNK), 0) % CHUNK
    c0 = lax.broadcasted_iota(jnp.int32, (2 * CHUNK, CHUNK), 1)
    k0 = k_ref[0:CHUNK, :]
    v0 = v_ref[0:CHUNK, :]
    carry = first_block(stack_q(q_ref[0:CHUNK, :]), k0, v0, (c0 <= r0) & (c0 >= PAD))
    o_ref[0:CHUNK, :] = finish(carry, CHUNK)

    meta_mask = lax.broadcasted_iota(jnp.int32, (2 * tq, CHUNK), 1) >= PAD
    rd = lax.broadcasted_iota(jnp.int32, (2 * tq, tq), 0) % tq
    cd = lax.broadcasted_iota(jnp.int32, (2 * tq, tq), 1)
    diag_mask = cd <= rd

    def q_block(i, _):
        row = pl.multiple_of(CHUNK + i * tq, CHUNK)
        qs = stack_q(q_ref[pl.ds(row, tq), :])
        carry = first_block(qs, k0, v0, meta_mask)

        def k_block(jb, carry):
            kr = pl.multiple_of(CHUNK + jb * tq, CHUNK)
            return next_block(carry, qs, k_ref[pl.ds(kr, tq), :], v_ref[pl.ds(kr, tq), :], None)

        carry = lax.fori_loop(0, i, k_block, carry)
        carry = next_block(carry, qs, k_ref[pl.ds(row, tq), :], v_ref[pl.ds(row, tq), :], diag_mask)
        o_ref[pl.ds(row, tq), :] = finish(carry, tq)
        return 0

    lax.fori_loop(0, n_blocks, q_block, 0)


MAX_SAFE_SCORE_BOUND = 60.0


def _diff_attention(proj, score_bound, lq1, lk1, lq2, lk2, subw, batch, lp, d_model,
                    col_q, col_k, col_v, lam_init):
    dv = d_model // DIFF_HEADS
    dh = dv // 2
    proj3 = proj.reshape(batch, lp, proj.shape[1])
    assert lp >= ATT_T and lp % ATT_T, "the kernel expects full blocks plus a short last block"
    small = lambda n: pl.BlockSpec((1, n), lambda b, h: (0, 0))
    head = lambda col: pl.BlockSpec((None, lp, dv), lambda b, h: (b, 0, col // dv + h))
    common = dict(
        grid=(batch, DIFF_HEADS),
        out_specs=pl.BlockSpec((None, lp, dv), lambda b, h: (b, 0, h)),
        out_shape=jax.ShapeDtypeStruct((batch, lp, d_model), BF16),
        compiler_params=pltpu.CompilerParams(
            dimension_semantics=("arbitrary", "arbitrary"), vmem_limit_bytes=VMEM_LIMIT),
    )
    tensors = [small(dh), small(dh), small(dh), small(dh), small(dv),
               head(col_q), head(col_k), head(col_v)]

    def bounded():
        return pl.pallas_call(
            functools.partial(_attn_bounded_kernel, lp, dh, lam_init),
            in_specs=[pl.BlockSpec(memory_space=pltpu.SMEM)] + tensors,
            scratch_shapes=[pltpu.VMEM((lp // ATT_T, 2 * ATT_T, dv), BF16),
                            pltpu.VMEM((lp, 2 * dv), BF16),
                            pltpu.VMEM((2 * ATT_T, ATT_T), BF16),
                            pltpu.VMEM((2 * ATT_T, ATT_T), BF16),
                            pltpu.VMEM((ATT_GROUP, 2 * ATT_T, 2 * dv), F32),
                            pltpu.VMEM((2, 2 * ATT_T, ATT_T), F32),
                            pltpu.VMEM((2 * (lp % ATT_T), lp), BF16)],
            name="diff_attn", **common,
        )(score_bound, lq1, lk1, lq2, lk2, subw, proj3, proj3, proj3)

    def online():
        return pl.pallas_call(
            functools.partial(_attn_online_kernel, lp, dh, lam_init),
            in_specs=tensors, name="diff_attn_online", **common,
        )(lq1, lk1, lq2, lk2, subw, proj3, proj3, proj3)

    out = lax.cond(score_bound[0] <= MAX_SAFE_SCORE_BOUND, bounded, online)
    return out.reshape(batch * lp, d_model)


def _merge_kernel(tiles_per_batch, ro_ref, do_ref, gr_ref, gd_ref, x_hbm, meta_ref, wr_ref, wd_ref,
                  wo_ref, n2_ref, h1_ref, u2_ref, hbuf_ref, sem_ref):
    step = pl.program_id(0)

    def fetch(tile, fn):
        slot = lax.rem(tile, 2)
        _padded_tile_copy(x_hbm, hbuf_ref.at[slot], sem_ref.at[slot], tile, tiles_per_batch, fn)

    @pl.when(step == 0)
    def _():
        fetch(step, lambda c: c.start())

    @pl.when(step + 1 < pl.num_programs(0))
    def _():
        fetch(step + 1, lambda c: c.start())

    fetch(step, lambda c: c.wait())
    slot = lax.rem(step, 2)

    @pl.when(lax.rem(step, tiles_per_batch) == 0)
    def _():
        hbuf_ref[slot, 0:CHUNK, :] = meta_ref[...]

    ro = _dot(ro_ref[...], wr_ref[...])
    do = _dot(do_ref[...], wd_ref[...])
    z = gr_ref[...].astype(F32) * ro + gd_ref[...].astype(F32) * do
    h1 = hbuf_ref[slot] + _dot(z.astype(BF16), wo_ref[...])
    h1_ref[...] = h1
    ms = jnp.mean(h1 * h1, axis=-1, keepdims=True)
    u2_ref[...] = (h1 * lax.rsqrt(ms + EPS) * n2_ref[...]).astype(BF16)


def _merge(ro, do, proj, x, meta_chunk, w_ret_o, w_diff_o, w_out, n2, col_gates, lp):
    batch, _, d_model = x.shape
    rows = batch * lp
    tm = _row_tile(lp, 640)
    gcol = col_gates // d_model
    const = lambda i: (0, 0)
    return pl.pallas_call(
        functools.partial(_merge_kernel, lp // tm),
        grid=(rows // tm,),
        in_specs=[
            pl.BlockSpec((tm, ro.shape[1]), lambda i: (i, 0)),
            pl.BlockSpec((tm, d_model), lambda i: (i, 0)),
            pl.BlockSpec((tm, d_model), lambda i: (i, gcol)),
            pl.BlockSpec((tm, d_model), lambda i: (i, gcol + 1)),
            pl.BlockSpec(memory_space=pl.ANY),
            pl.BlockSpec((CHUNK, d_model), const),
            pl.BlockSpec(w_ret_o.shape, const),
            pl.BlockSpec(w_diff_o.shape, const),
            pl.BlockSpec(w_out.shape, const),
            pl.BlockSpec((1, d_model), const),
        ],
        out_specs=[pl.BlockSpec((tm, d_model), lambda i: (i, 0)),
                   pl.BlockSpec((tm, d_model), lambda i: (i, 0))],
        out_shape=[jax.ShapeDtypeStruct((rows, d_model), F32),
                   jax.ShapeDtypeStruct((rows, d_model), BF16)],
        scratch_shapes=[pltpu.VMEM((2, tm, d_model), x.dtype),
                        pltpu.SemaphoreType.DMA((2,))],
        compiler_params=pltpu.CompilerParams(
            dimension_semantics=("arbitrary",), vmem_limit_bytes=VMEM_LIMIT),
        name="merge",
    )(ro, do, proj, proj, x, meta_chunk, w_ret_o, w_diff_o, w_out, n2)


def _ffn_kernel(ffn, tc, tiles_per_batch, u_ref, h_ref, wu_ref, cw_ref, cb_ref, wd_ref, o_hbm,
                carry_ref, up_ref, act_ref, obuf_ref, sem_ref):
    tm = u_ref.shape[0]
    halo = carry_ref.shape[0]
    step = pl.program_id(0)
    n_steps = pl.num_programs(0)

    def out_copy(s, first):
        slot = lax.rem(s, 2)
        b = lax.div(s, tiles_per_batch)
        if first:
            return pltpu.make_async_copy(obuf_ref.at[slot, pl.ds(CHUNK, tm - CHUNK)],
                                         o_hbm.at[b, pl.ds(0, tm - CHUNK)], sem_ref.at[slot])
        start = pl.multiple_of(lax.rem(s, tiles_per_batch) * tm - CHUNK, CHUNK)
        return pltpu.make_async_copy(obuf_ref.at[slot], o_hbm.at[b, pl.ds(start, tm)], sem_ref.at[slot])

    def for_copy(s, fn):
        first = lax.rem(s, tiles_per_batch) == 0

        @pl.when(first)
        def _():
            fn(out_copy(s, True))

        @pl.when(jnp.logical_not(first))
        def _():
            fn(out_copy(s, False))

    @pl.when(step == 0)
    def _():
        carry_ref[...] = jnp.zeros_like(carry_ref)

    @pl.when(step >= 2)
    def _():
        for_copy(step - 2, lambda c: c.wait())

    u = u_ref[...]

    def conv_cols(col):
        up_ref[0:halo, :] = carry_ref[:, col:col + tc]
        up_ref[halo:halo + tm, :] = _dot(u, wu_ref[:, col:col + tc])
        carry_ref[:, col:col + tc] = up_ref[tm:tm + halo, :]
        y = cb_ref[:, col:col + tc]
        for t in range(CONV_W):
            lo = halo - (CONV_W - 1) + t
            y = y + up_ref[lo:lo + tm, :] * cw_ref[t:t + 1, col:col + tc]
        return y

    for c in range(ffn // tc):
        a = conv_cols(c * tc)
        b = conv_cols(ffn + c * tc)
        act_ref[:, c * tc:(c + 1) * tc] = (a * _sigmoid(a) * b).astype(BF16)

    obuf_ref[lax.rem(step, 2)] = h_ref[...] + _dot(act_ref[...], wd_ref[...])
    for_copy(step, lambda c: c.start())

    @pl.when(step == n_steps - 1)
    def _():
        @pl.when(step >= 1)
        def _():
            for_copy(step - 1, lambda c: c.wait())

        for_copy(step, lambda c: c.wait())


def _ffn(u2, h1, w_up, conv_w, conv_b, w_down, batch, lp):
    rows, d_model = h1.shape
    ffn = w_down.shape[0]
    tc = MXU
    halo = 8
    tm = _row_tile(lp, 640)
    const = lambda i: (0, 0)
    resident = lambda shape: pl.BlockSpec(shape, const, pipeline_mode=pl.Buffered(1))
    kern = functools.partial(_ffn_kernel, ffn, tc, lp // tm)
    return pl.pallas_call(
        kern,
        grid=(rows // tm,),
        in_specs=[
            pl.BlockSpec((tm, d_model), lambda i: (i, 0)),
            pl.BlockSpec((tm, d_model), lambda i: (i, 0)),
            resident(w_up.shape),
            pl.BlockSpec(conv_w.shape, const),
            pl.BlockSpec(conv_b.shape, const),
            resident(w_down.shape),
        ],
        out_specs=pl.BlockSpec(memory_space=pl.ANY),
        out_shape=jax.ShapeDtypeStruct((batch, lp - CHUNK, d_model), F32),
        scratch_shapes=[pltpu.VMEM((halo, 2 * ffn), F32),
                        pltpu.VMEM((tm + halo, tc), F32),
                        pltpu.VMEM((tm, ffn), BF16),
                        pltpu.VMEM((2, tm, d_model), F32),
                        pltpu.SemaphoreType.DMA((2,))],
        compiler_params=pltpu.CompilerParams(
            dimension_semantics=("arbitrary",), vmem_limit_bytes=VMEM_LIMIT),
        name="conv_ffn",
    )(u2, h1, w_up, conv_w, conv_b, w_down)


def _rope_tables(lp, ret_dk, diff_dh):
    f32 = np.float32
    pos = np.arange(lp, dtype=f32) - f32(PAD)
    half = ret_dk // 2
    inv = np.power(f32(RET_THETA), -np.arange(half, dtype=f32) / f32(half))
    ang = pos[:, None] * inv[None, :]
    rcos, rsin = np.cos(ang), np.sin(ang)
    rot = diff_dh // 8
    inv = np.power(f32(ROPE_THETA), -np.arange(rot, dtype=f32) / f32(rot))
    ang = pos[:, None] * inv[None, :]
    cos, sin = np.cos(ang), np.sin(ang)
    ones = np.ones((lp, diff_dh - 2 * rot), f32)
    zeros = np.zeros((lp, diff_dh - 2 * rot), f32)
    zr = np.zeros((lp, rot), f32)
    rep = LANE // diff_dh
    dc = np.tile(np.concatenate([cos, cos, ones], axis=1), (1, rep))
    ds1 = np.tile(np.concatenate([-sin, zr, zeros], axis=1), (1, rep))
    ds2 = np.tile(np.concatenate([zr, sin, zeros], axis=1), (1, rep))
    return rcos, rsin, (dc, ds1, ds2), rot


def _fold_norm_weight(tables, rot, w, diff_dh, scale):
    dc, ds1, ds2 = tables
    w128 = jnp.tile(w.astype(F32).reshape(-1), LANE // diff_dh) * scale
    return jnp.stack([dc * w128, ds1 * jnp.roll(w128, -rot), ds2 * jnp.roll(w128, rot)])


def kernel(x, meta_tokens, norm1_w, w_in, w_ret_o, q_norm_w, k_norm_w, lambda_q1, lambda_k1,
           lambda_q2, lambda_k2, diff_subln_w, w_diff_o, w_out, norm2_w, w_up, conv_w, conv_b,
           w_down):
    batch, seq, d_model = x.shape
    depth = norm1_w.shape[0]
    lp = CHUNK + seq
    ret_dk = d_model // RET_HEADS
    diff_dv = d_model // DIFF_HEADS
    diff_dh = diff_dv // 2
    assert (lp - CHUNK) % ATT_TQ == 0 and d_model % MXU == 0 and ret_dk // 2 == LANE

    seg_of_tile = ("rq", "rk", "plain", "plain", "silu", "silu", "dq", "dk", "plain",
                   "sigmoid", "sigmoid")
    col_rq, col_rk, col_rv, col_rg = 0, d_model, 2 * d_model, 4 * d_model
    col_dq, col_dk, col_dv, col_gates = 6 * d_model, 7 * d_model, 8 * d_model, 9 * d_model
    assert w_in.shape[2] == len(seg_of_tile) * d_model

    rcos, rsin, diff_tables, rot = _rope_tables(lp, ret_dk, diff_dh)
    gid = np.arange(MXU) // diff_dh
    gmat = jnp.asarray(gid[:, None] == gid[None, :], dtype=BF16)

    meta_chunk = jnp.concatenate([jnp.zeros((PAD, d_model), x.dtype), meta_tokens.astype(x.dtype)], axis=0)

    assert depth == 1, "a second layer would need the pad rows of the residual re-zeroed"
    for l in range(depth):
        lam_init = 0.8 - 0.6 * math.exp(-0.3 * l)
        row = lambda a: a[l].astype(F32).reshape(1, -1)
        qtab = _fold_norm_weight(diff_tables, rot, q_norm_w[l], diff_dh, diff_dh ** -0.5 * LOG2E)
        ktab = _fold_norm_weight(diff_tables, rot, k_norm_w[l], diff_dh, 1.0)
        proj = _input_projection(x, meta_chunk, row(norm1_w), w_in[l].astype(BF16), rcos, rsin,
                                 qtab, ktab, gmat, seg_of_tile, lp, ret_dk, diff_dh)
        ro = _retention(proj, batch, lp, d_model, col_rq, col_rk, col_rv, col_rg)
        score_bound = (jnp.max(jnp.abs(q_norm_w[l].astype(F32))) * jnp.max(jnp.abs(k_norm_w[l].astype(F32)))
                       * (diff_dh ** 0.5 * LOG2E * (1.0 + 2.0 ** -5))).reshape(1)
        do = _diff_attention(proj, score_bound, row(lambda_q1), row(lambda_k1), row(lambda_q2),
                             row(lambda_k2), row(diff_subln_w), batch, lp, d_model,
                             col_dq, col_dk, col_dv, lam_init)
        h1, u2 = _merge(ro, do, proj, x, meta_chunk, w_ret_o[l].astype(BF16),
                        w_diff_o[l].astype(BF16), w_out[l].astype(BF16), row(norm2_w), col_gates, lp)
        out = _ffn(u2, h1, w_up[l].astype(BF16), conv_w[l].astype(F32),
                   conv_b[l].astype(F32).reshape(1, -1), w_down[l].astype(BF16), batch, lp)

    return out
```
